```python
import math
import jax
import jax.numpy as jnp
from jax import lax
import numpy as np

D_MODEL = 4096
BATCH = 4
SEQ = 2048
DEPTH = 4
DEC_BATCH = 8
DEC_SEQ = 8
PAST_LEN = 8192
PAGE_SIZE = 128

N_EVEN = (DEPTH + 1) // 2
N_ODD = DEPTH // 2
D_PLE = 256
D_FF = 4 * D_MODEL
EPS = 1e-6
NEG_INF = -1e30

POOL_WIDTH = D_MODEL // 2
POOL_WINDOWS = (2, 4, 8, 16)
POOL_GROUPS = len(POOL_WINDOWS)
POOL_GC = POOL_WIDTH // POOL_GROUPS
POOL_BUF = max(POOL_WINDOWS) - 1
DA_HEADS = 8
DA_HEAD_DIM = 128
DA_WIDTH = DA_HEADS * 2 * DA_HEAD_DIM
Q_BLOCK = 128
EVEN_IN = POOL_WIDTH + 3 * DA_WIDTH
EVEN_MIX = POOL_WIDTH + DA_WIDTH
CONV_WIDTH = D_MODEL // 2
CONV_K = 31
SSM_INNER = D_MODEL // 2
SSM_HEAD_DIM = 64
SSM_HEADS = SSM_INNER // SSM_HEAD_DIM
SSM_GROUPS = 8
SSM_STATE = 128
SSM_CONV_K = 4
SSM_CHUNK = 128
SSM_GN = SSM_GROUPS * SSM_STATE
SSM_XBC = SSM_INNER + 2 * SSM_GN
ODD_IN = 2 * CONV_WIDTH + SSM_INNER + SSM_XBC + SSM_HEADS
ODD_MIX = CONV_WIDTH + SSM_INNER

kernel_name = 'hybrid_pool_diffattn_conformer_ssd_decoder_step'


def rms_norm(x, g):
    xf = x.astype(jnp.float32)
    y = xf * lax.rsqrt(jnp.mean(xf * xf, axis=-1, keepdims=True) + EPS)
    return (y * g.astype(jnp.float32)).astype(x.dtype)


def layer_norm(x, g, b):
    xf = x.astype(jnp.float32)
    mu = jnp.mean(xf, axis=-1, keepdims=True)
    var = jnp.mean(jnp.square(xf - mu), axis=-1, keepdims=True)
    y = (xf - mu) * lax.rsqrt(var + EPS)
    return (y * g.astype(jnp.float32) + b.astype(jnp.float32)).astype(x.dtype)


def causal_dwconv(x, buf, w):
    k = w.shape[0]
    full = jnp.concatenate([buf.astype(x.dtype), x], axis=1)
    y = lax.conv_general_dilated(full, w[:, None, :].astype(x.dtype), (1,), 'VALID',
                                 dimension_numbers=('NWC', 'WIO', 'NWC'),
                                 feature_group_count=x.shape[-1])
    return y, full[:, full.shape[1] - (k - 1):]


def pool_mixer(u, buf, pos0, w_grp, scale):
    bsz, t = u.shape[:2]
    full = jnp.concatenate([buf.astype(u.dtype), u], axis=1)
    ff = full.astype(jnp.float32)
    cs = jnp.concatenate([jnp.zeros((bsz, 1, POOL_WIDTH), jnp.float32), jnp.cumsum(ff, axis=1)], axis=1)
    pos = pos0 + jnp.arange(t)
    hi = cs[:, POOL_BUF + 1:]
    means = []
    for g, w in enumerate(POOL_WINDOWS):
        sl = slice(g * POOL_GC, (g + 1) * POOL_GC)
        lo = cs[:, POOL_BUF + 1 - w:POOL_BUF + 1 - w + t, sl]
        cnt = jnp.minimum(pos + 1, w).astype(jnp.float32)[None, :, None]
        means.append((hi[..., sl] - lo) / cnt)
    d = (jnp.concatenate(means, axis=-1) - ff[:, POOL_BUF:]).astype(u.dtype)
    z = jnp.einsum('btgc,gcd->btgd', d.reshape(bsz, t, POOL_GROUPS, POOL_GC), w_grp)
    return z.reshape(bsz, t, POOL_WIDTH) * scale, full[:, t:]


def alibi_slopes(n):
    return jnp.power(2.0, -8.0 * jnp.arange(1, n + 1, dtype=jnp.float32) / n)


def diff_attention(q, k, v, q_pos, k_pos, lam):
    bsz, tq = q.shape[:2]
    qb = Q_BLOCK if tq % Q_BLOCK == 0 else tq
    nb = tq // qb
    slopes = alibi_slopes(DA_HEADS)
    scale = DA_HEAD_DIM ** -0.5

    def block(args):
        qblk, qp = args
        s = jnp.einsum('bqhcd,bkhcd->cbhqk', qblk, k).astype(jnp.float32) * scale
        dist = (qp[:, None] - k_pos[None, :]).astype(jnp.float32)
        s = jnp.where(dist >= 0, s - slopes[:, None, None] * dist, NEG_INF)
        p = jax.nn.softmax(s, axis=-1)
        a = (p[0] - lam * p[1]).astype(v.dtype)
        return jnp.einsum('bhqk,bkhe->bqhe', a, v)

    qs = q.reshape(bsz, nb, qb, DA_HEADS, 2, DA_HEAD_DIM).swapaxes(0, 1)
    o = lax.map(block, (qs, q_pos.reshape(nb, qb)))
    return o.swapaxes(0, 1).reshape(bsz, tq, DA_HEADS, 2 * DA_HEAD_DIM)


def segsum(a):
    n = a.shape[-1]
    cs = jnp.cumsum(a, axis=-1)
    diff = cs[..., :, None] - cs[..., None, :]
    return jnp.where(jnp.tril(jnp.ones((n, n), bool)), diff, -jnp.inf)


def ssd_scan(x, dt, a, bm, cm, h0):
    bsz, t = x.shape[:2]
    L = SSM_CHUNK if t % SSM_CHUNK == 0 else t
    nc = t // L
    r = SSM_HEADS // SSM_GROUPS
    f32 = jnp.float32
    xd = (x.astype(f32) * dt[..., None]).reshape(bsz, nc, L, SSM_GROUPS, r, SSM_HEAD_DIM)
    ad = (dt * a).reshape(bsz, nc, L, SSM_GROUPS, r).transpose(0, 3, 4, 1, 2)
    bc = bm.astype(f32).reshape(bsz, nc, L, SSM_GROUPS, SSM_STATE)
    cc = cm.astype(f32).reshape(bsz, nc, L, SSM_GROUPS, SSM_STATE)
    a_cs = jnp.cumsum(ad, axis=-1)
    decay = jnp.exp(segsum(ad))
    scores = jnp.einsum('bclgn,bcsgn->bgcls', cc, bc)
    y_diag = jnp.einsum('bgrcls,bcsgrp->bclgrp', scores[:, :, None] * decay, xd)
    states = jnp.einsum('bcsgn,bgrcs,bcsgrp->bcgrpn', bc, jnp.exp(a_cs[..., -1:] - a_cs), xd)
    chunk_decay = jnp.exp(a_cs[..., -1])

    def step(h, inp):
        st, dec = inp
        return h * dec[..., None, None] + st, h

    h_init = h0.astype(f32).reshape(bsz, SSM_GROUPS, r, SSM_HEAD_DIM, SSM_STATE)
    h_last, h_prev = lax.scan(step, h_init, (jnp.moveaxis(states, 1, 0), jnp.moveaxis(chunk_decay, -1, 0)))
    y_off = jnp.einsum('bclgn,cbgrpn,bgrcl->bclgrp', cc, h_prev, jnp.exp(a_cs))
    y = (y_diag + y_off).reshape(bsz, t, SSM_HEADS, SSM_HEAD_DIM)
    return y, h_last.reshape(bsz, SSM_HEADS, SSM_HEAD_DIM, SSM_STATE)


def even_mixer(hn, layer_idx, j, pool_buf, past, W):
    bsz, t, _ = hn.shape
    proj = hn @ W['w_in_even'][j]
    u = proj[..., :POOL_WIDTH]
    q = proj[..., POOL_WIDTH:POOL_WIDTH + DA_WIDTH].reshape(bsz, t, DA_HEADS, 2, DA_HEAD_DIM)
    k_rows = proj[..., POOL_WIDTH + DA_WIDTH:POOL_WIDTH + 2 * DA_WIDTH].reshape(bsz, t, DA_HEADS, 2 * DA_HEAD_DIM)
    v_rows = proj[..., POOL_WIDTH + 2 * DA_WIDTH:].reshape(bsz, t, DA_HEADS, 2 * DA_HEAD_DIM)
    pos0 = 0 if past is None else past[0].shape[1]
    z_pool, pool_new = pool_mixer(u, pool_buf, pos0, W['pool_w'][j], W['pool_scale'][j])
    if past is None:
        k_all, v_all = k_rows, v_rows
    else:
        k_all = jnp.concatenate([past[0].astype(k_rows.dtype), k_rows], axis=1)
        v_all = jnp.concatenate([past[1].astype(v_rows.dtype), v_rows], axis=1)
    lam_init = 0.8 - 0.6 * math.exp(-0.3 * layer_idx)
    f32 = jnp.float32
    lam = (jnp.exp(jnp.sum(W['lambda_q1'][j].astype(f32) * W['lambda_k1'][j].astype(f32)))
           - jnp.exp(jnp.sum(W['lambda_q2'][j].astype(f32) * W['lambda_k2'][j].astype(f32))) + lam_init)
    q_pos = pos0 + jnp.arange(t)
    k_pos = jnp.arange(pos0 + t)
    o = diff_attention(q, k_all.reshape(bsz, pos0 + t, DA_HEADS, 2, DA_HEAD_DIM), v_all, q_pos, k_pos, lam)
    o = rms_norm(o, W['subln_g'][j]) * (1.0 - lam_init)
    mix = jnp.concatenate([z_pool, o.reshape(bsz, t, DA_WIDTH).astype(z_pool.dtype)], axis=-1) @ W['w_out_even'][j]
    return mix, k_rows, v_rows, pool_new


def odd_mixer(hn, j, conf_buf, mconv_buf, h0, W):
    bsz, t, _ = hn.shape
    f32 = jnp.float32
    proj = hn @ W['w_in_odd'][j]
    o0 = 2 * CONV_WIDTH
    ca = proj[..., :CONV_WIDTH]
    cg = proj[..., CONV_WIDTH:o0]
    z = proj[..., o0:o0 + SSM_INNER]
    xbc = proj[..., o0 + SSM_INNER:o0 + SSM_INNER + SSM_XBC]
    dt_raw = proj[..., o0 + SSM_INNER + SSM_XBC:]
    glu = ca * jax.nn.sigmoid(cg)
    cc, conf_new = causal_dwconv(glu, conf_buf, W['conf_dw_w'][j])
    cc = jax.nn.silu(layer_norm(cc + W['conf_dw_b'][j], W['conf_ln_g'][j], W['conf_ln_b'][j]))
    zc = cc @ W['conf_pw_w'][j] + W['conf_pw_b'][j]
    xc, mconv_new = causal_dwconv(xbc, mconv_buf, W['ssm_conv_w'][j])
    xc = jax.nn.silu(xc + W['ssm_conv_b'][j])
    xs = xc[..., :SSM_INNER].reshape(bsz, t, SSM_HEADS, SSM_HEAD_DIM)
    bm = xc[..., SSM_INNER:SSM_INNER + SSM_GN].reshape(bsz, t, SSM_GROUPS, SSM_STATE)
    cm = xc[..., SSM_INNER + SSM_GN:].reshape(bsz, t, SSM_GROUPS, SSM_STATE)
    dt = jax.nn.softplus(dt_raw.astype(f32) + W['ssm_dt_bias'][j].astype(f32))
    a = -jnp.exp(W['ssm_A_log'][j].astype(f32))
    y, h_new = ssd_scan(xs, dt, a, bm, cm, h0)
    y = y + W['ssm_D'][j].astype(f32)[:, None] * xs.astype(f32)
    y = y.reshape(bsz, t, SSM_INNER) * jax.nn.silu(z.astype(f32))
    y = rms_norm(y.reshape(bsz, t, SSM_GROUPS, SSM_INNER // SSM_GROUPS),
                 W['ssm_norm_g'][j].reshape(SSM_GROUPS, SSM_INNER // SSM_GROUPS))
    y = y.reshape(bsz, t, SSM_INNER).astype(zc.dtype)
    mix = jnp.concatenate([zc, y], axis=-1) @ W['w_out_odd'][j]
    return mix, conf_new, mconv_new, h_new.astype(h0.dtype)


def trunk(x, pemb, W, pool_buf, conf_buf, mconv_buf, ssm_h0, cache_k=None, cache_v=None, page_table=None):
    bsz = x.shape[0]
    h = x
    k_new, v_new, pool_new, conf_new, mconv_new, ssm_new = [], [], [], [], [], []
    for i in range(DEPTH):
        j = i // 2
        hn = rms_norm(h, W['g_mix'][i])
        if i % 2 == 0:
            past = None
            if cache_k is not None:
                n_past = page_table.shape[1] * cache_k.shape[2]
                past = (cache_k[j][page_table].reshape(bsz, n_past, DA_HEADS, 2 * DA_HEAD_DIM),
                        cache_v[j][page_table].reshape(bsz, n_past, DA_HEADS, 2 * DA_HEAD_DIM))
            mix, kr, vr, pb = even_mixer(hn, i, j, pool_buf[j], past, W)
            k_new.append(kr)
            v_new.append(vr)
            pool_new.append(pb)
        else:
            mix, cb, mb, hs = odd_mixer(hn, j, conf_buf[j], mconv_buf[j], ssm_h0[j], W)
            conf_new.append(cb)
            mconv_new.append(mb)
            ssm_new.append(hs)
        h = h + mix
        hm = rms_norm(h, W['g_mlp'][i])
        h = h + jnp.square(jax.nn.relu(hm @ W['w_up'][i])) @ W['w_down'][i]
        gate = jax.nn.sigmoid(rms_norm(h, W['g_ple'][i]) @ W['w_ple_gate'][i])
        h = h + gate * (pemb[i].astype(h.dtype) @ W['w_ple_proj'][i])
    return (rms_norm(h, W['g_final']), jnp.stack(k_new), jnp.stack(v_new), jnp.stack(pool_new),
            jnp.stack(conf_new), jnp.stack(mconv_new), jnp.stack(ssm_new))


def setup_inputs(seed: int = 0) -> dict:
    key = jax.random.key(seed)
    ks = iter(jax.random.split(key, 64))
    f32 = jnp.float32

    def nrm(shape, scale):
        return jax.random.normal(next(ks), shape, f32) * scale

    def gain(shape):
        return 1.0 + nrm(shape, 0.02)

    n_pages = PAST_LEN // PAGE_SIZE
    n_used = DEC_BATCH * n_pages
    n_pool = n_used + max(1, n_used // 4)
    page_table = jax.random.permutation(next(ks), n_pool)[:n_used].reshape(DEC_BATCH, n_pages).astype(jnp.int32)
    dt0 = jnp.exp(jax.random.uniform(next(ks), (N_ODD, SSM_HEADS), f32, math.log(1e-3), math.log(1e-1)))
    dt_bias = dt0 + jnp.log(-jnp.expm1(-dt0))
    a_log = jnp.log(jax.random.uniform(next(ks), (N_ODD, SSM_HEADS), f32, 1.0, 16.0))
    return {
        'x_prompt': nrm((BATCH, SEQ, D_MODEL), 1.0),
        'x_sample': nrm((DEC_BATCH, DEC_SEQ, D_MODEL), 1.0),
        'p_prompt': nrm((DEPTH, BATCH, SEQ, D_PLE), 1.0),
        'p_sample': nrm((DEPTH, DEC_BATCH, DEC_SEQ, D_PLE), 1.0),
        'cache_k': nrm((N_EVEN, n_pool, PAGE_SIZE, DA_HEADS, 2 * DA_HEAD_DIM), 1.0),
        'cache_v': nrm((N_EVEN, n_pool, PAGE_SIZE, DA_HEADS, 2 * DA_HEAD_DIM), 1.0),
        'page_table': page_table,
        'state_pool': nrm((N_EVEN, DEC_BATCH, POOL_BUF, POOL_WIDTH), 1.0),
        'state_conf_conv': nrm((N_ODD, DEC_BATCH, CONV_K - 1, CONV_WIDTH), 0.5),
        'state_ssm_conv': nrm((N_ODD, DEC_BATCH, SSM_CONV_K - 1, SSM_XBC), 1.0),
        'state_ssm': nrm((N_ODD, DEC_BATCH, SSM_HEADS, SSM_HEAD_DIM, SSM_STATE), 0.1),
        'g_mix': gain((DEPTH, D_MODEL)),
        'g_mlp': gain((DEPTH, D_MODEL)),
        'g_ple': gain((DEPTH, D_MODEL)),
        'g_final': gain((D_MODEL,)),
        'w_in_even': nrm((N_EVEN, D_MODEL, EVEN_IN), D_MODEL ** -0.5),
        'pool_w': nrm((N_EVEN, POOL_GROUPS, POOL_GC, POOL_GC), POOL_GC ** -0.5),
        'pool_scale': gain((N_EVEN, POOL_WIDTH)),
        'lambda_q1': nrm((N_EVEN, DA_HEAD_DIM), 0.1),
        'lambda_k1': nrm((N_EVEN, DA_HEAD_DIM), 0.1),
        'lambda_q2': nrm((N_EVEN, DA_HEAD_DIM), 0.1),
        'lambda_k2': nrm((N_EVEN, DA_HEAD_DIM), 0.1),
        'subln_g': gain((N_EVEN, 2 * DA_HEAD_DIM)),
        'w_out_even': nrm((N_EVEN, EVEN_MIX, D_MODEL), EVEN_MIX ** -0.5),
        'w_in_odd': nrm((N_ODD, D_MODEL, ODD_IN), D_MODEL ** -0.5),
        'conf_dw_w': nrm((N_ODD, CONV_K, CONV_WIDTH), CONV_K ** -0.5),
        'conf_dw_b': nrm((N_ODD, CONV_WIDTH), 0.02),
        'conf_ln_g': gain((N_ODD, CONV_WIDTH)),
        'conf_ln_b': nrm((N_ODD, CONV_WIDTH), 0.02),
        'conf_pw_w': nrm((N_ODD, CONV_WIDTH, CONV_WIDTH), CONV_WIDTH ** -0.5),
        'conf_pw_b': nrm((N_ODD, CONV_WIDTH), 0.02),
        'ssm_conv_w': nrm((N_ODD, SSM_CONV_K, SSM_XBC), SSM_CONV_K ** -0.5),
        'ssm_conv_b': nrm((N_ODD, SSM_XBC), 0.02),
        'ssm_dt_bias': dt_bias,
        'ssm_A_log': a_log,
        'ssm_D': gain((N_ODD, SSM_HEADS)),
        'ssm_norm_g': gain((N_ODD, SSM_INNER)),
        'w_out_odd': nrm((N_ODD, ODD_MIX, D_MODEL), ODD_MIX ** -0.5),
        'w_up': nrm((DEPTH, D_MODEL, D_FF), D_MODEL ** -0.5),
        'w_down': nrm((DEPTH, D_FF, D_MODEL), D_FF ** -0.5),
        'w_ple_proj': nrm((DEPTH, D_PLE, D_MODEL), D_PLE ** -0.5),
        'w_ple_gate': nrm((DEPTH, D_MODEL, D_MODEL), D_MODEL ** -0.5),
    }


def reference(x_prompt, x_sample, p_prompt, p_sample, cache_k, cache_v, page_table, state_pool,
              state_conf_conv, state_ssm_conv, state_ssm, g_mix, g_mlp, g_ple, g_final, w_in_even, pool_w,
              pool_scale, lambda_q1, lambda_k1, lambda_q2, lambda_k2, subln_g, w_out_even, w_in_odd,
              conf_dw_w, conf_dw_b, conf_ln_g, conf_ln_b, conf_pw_w, conf_pw_b, ssm_conv_w, ssm_conv_b,
              ssm_dt_bias, ssm_A_log, ssm_D, ssm_norm_g, w_out_odd, w_up, w_down, w_ple_proj, w_ple_gate):
    W = dict(g_mix=g_mix, g_mlp=g_mlp, g_ple=g_ple, g_final=g_final, w_in_even=w_in_even, pool_w=pool_w,
             pool_scale=pool_scale, lambda_q1=lambda_q1, lambda_k1=lambda_k1, lambda_q2=lambda_q2,
             lambda_k2=lambda_k2, subln_g=subln_g, w_out_even=w_out_even, w_in_odd=w_in_odd,
             conf_dw_w=conf_dw_w, conf_dw_b=conf_dw_b, conf_ln_g=conf_ln_g, conf_ln_b=conf_ln_b,
             conf_pw_w=conf_pw_w, conf_pw_b=conf_pw_b, ssm_conv_w=ssm_conv_w, ssm_conv_b=ssm_conv_b,
             ssm_dt_bias=ssm_dt_bias, ssm_A_log=ssm_A_log, ssm_D=ssm_D, ssm_norm_g=ssm_norm_g,
             w_out_odd=w_out_odd, w_up=w_up, w_down=w_down, w_ple_proj=w_ple_proj, w_ple_gate=w_ple_gate)
    bp = x_prompt.shape[0]
    dtp = x_prompt.dtype
    zero_pool = jnp.zeros((N_EVEN, bp, POOL_BUF, POOL_WIDTH), dtp)
    zero_conf = jnp.zeros((N_ODD, bp, CONV_K - 1, CONV_WIDTH), dtp)
    zero_mconv = jnp.zeros((N_ODD, bp, SSM_CONV_K - 1, SSM_XBC), dtp)
    zero_ssm = jnp.zeros((N_ODD, bp, SSM_HEADS, SSM_HEAD_DIM, SSM_STATE), dtp)
    y_prompt, k_prompt, v_prompt, pool_prompt, conf_prompt, mconv_prompt, ssm_prompt = trunk(
        x_prompt, p_prompt, W, zero_pool, zero_conf, zero_mconv, zero_ssm)
    y_sample, k_sample, v_sample, pool_sample, conf_sample, mconv_sample, ssm_sample = trunk(
        x_sample, p_sample, W, state_pool, state_conf_conv, state_ssm_conv, state_ssm,
        cache_k, cache_v, page_table)
    return (y_prompt, y_sample, k_prompt, v_prompt, k_sample, v_sample, pool_prompt, pool_sample,
            conf_prompt, conf_sample, mconv_prompt, mconv_sample, ssm_prompt, ssm_sample)
```

```python
import functools
import math

import jax
import jax.numpy as jnp
from jax import lax
from jax.experimental import pallas as pl
from jax.experimental.pallas import tpu as pltpu

F32 = jnp.float32
BF16 = jnp.bfloat16
EPS = 1e-6
NEG_INF = -1e30

V7X_VMEM_BYTES = 64 * 1024 * 1024
VMEM_LIMIT = V7X_VMEM_BYTES - 4 * 1024 * 1024

POOL_WINDOWS = (2, 4, 8, 16)
POOL_BUF = max(POOL_WINDOWS) - 1
POOL_CARRY = 16
CONV_K = 31
CONV_CARRY = 32
SSM_CONV_K = 4
SSM_CARRY = 8
SSM_CHUNK = 128
DA_HEADS = 8
DA_HEAD_DIM = 128
SSM_HEAD_DIM = 64
SSM_GROUPS = 8
SSM_STATE = 128
SAMPLE_T_PAD = 16


def _cparams(n_axes):
    return pltpu.CompilerParams(
        dimension_semantics=("arbitrary",) * n_axes, vmem_limit_bytes=VMEM_LIMIT)


def _rms_rows(x, g):
    ms = jnp.mean(x * x, axis=-1, keepdims=True)
    return x * lax.rsqrt(ms + EPS) * g


def _norm_mm_kernel(x_ref, g_ref, w_ref, o_ref, xn_ref, *, act):
    @pl.when(pl.program_id(1) == 0)
    def _():
        xn_ref[...] = _rms_rows(x_ref[...], g_ref[...]).astype(BF16)

    acc = jnp.dot(xn_ref[...], w_ref[...], preferred_element_type=F32)
    if act == "relu2":
        acc = jnp.square(jnp.maximum(acc, 0.0))
    o_ref[...] = acc.astype(o_ref.dtype)


def norm_matmul(x, g, w, *, tm, tn, act=None, out_dtype=F32):
    m, k = x.shape
    n = w.shape[1]
    assert m % tm == 0 and n % tn == 0
    return pl.pallas_call(
        functools.partial(_norm_mm_kernel, act=act),
        grid=(m // tm, n // tn),
        in_specs=[
            pl.BlockSpec((tm, k), lambda i, j: (i, 0)),
            pl.BlockSpec((1, k), lambda i, j: (0, 0)),
            pl.BlockSpec((k, tn), lambda i, j: (0, j)),
        ],
        out_specs=pl.BlockSpec((tm, tn), lambda i, j: (i, j)),
        out_shape=jax.ShapeDtypeStruct((m, n), out_dtype),
        scratch_shapes=[pltpu.VMEM((tm, k), BF16)],
        compiler_params=_cparams(2),
        name="norm_matmul",
    )(x, g.reshape(1, k), w)


def _mm2_res_kernel(a1_ref, a2_ref, w_ref, r_ref, o_ref):
    k1 = a1_ref.shape[1]
    acc = jnp.dot(a1_ref[...], w_ref[:k1, :], preferred_element_type=F32)
    acc += jnp.dot(a2_ref[...], w_ref[k1:, :], preferred_element_type=F32)
    o_ref[...] = r_ref[...] + acc


def matmul2_residual(a1, a2, w, res, *, tm, tn):
    m, k1 = a1.shape
    k2 = a2.shape[1]
    n = w.shape[1]
    assert m % tm == 0 and n % tn == 0 and w.shape[0] == k1 + k2
    return pl.pallas_call(
        _mm2_res_kernel,
        grid=(m // tm, n // tn),
        in_specs=[
            pl.BlockSpec((tm, k1), lambda i, j: (i, 0)),
            pl.BlockSpec((tm, k2), lambda i, j: (i, 0)),
            pl.BlockSpec((k1 + k2, tn), lambda i, j: (0, j)),
            pl.BlockSpec((tm, tn), lambda i, j: (i, j)),
        ],
        out_specs=pl.BlockSpec((tm, tn), lambda i, j: (i, j)),
        out_shape=jax.ShapeDtypeStruct((m, n), F32),
        compiler_params=_cparams(2),
        name="matmul2_residual",
    )(a1, a2, w, res)


def _mmk_res_kernel(a_ref, w_ref, r_ref, o_ref):
    part = jnp.dot(a_ref[...], w_ref[...], preferred_element_type=F32)

    @pl.when(pl.program_id(2) == 0)
    def _():
        o_ref[...] = r_ref[...] + part

    @pl.when(pl.program_id(2) != 0)
    def _():
        o_ref[...] += part


def matmul_k_residual(a, w, res, *, tm, tn, tk):
    m, k = a.shape
    n = w.shape[1]
    assert m % tm == 0 and n % tn == 0 and k % tk == 0
    return pl.pallas_call(
        _mmk_res_kernel,
        grid=(m // tm, n // tn, k // tk),
        in_specs=[
            pl.BlockSpec((tm, tk), lambda i, j, l: (i, l)),
            pl.BlockSpec((tk, tn), lambda i, j, l: (l, j)),
            pl.BlockSpec((tm, tn), lambda i, j, l: (i, j)),
        ],
        out_specs=pl.BlockSpec((tm, tn), lambda i, j, l: (i, j)),
        out_shape=jax.ShapeDtypeStruct((m, n), F32),
        compiler_params=_cparams(3),
        name="matmul_k_residual",
    )(a, w, res)


def _ple_kernel(x_ref, g_ref, wg_ref, pe_ref, wp_ref, r_ref, o_ref, xn_ref):
    @pl.when(pl.program_id(1) == 0)
    def _():
        xn_ref[...] = _rms_rows(x_ref[...], g_ref[...]).astype(BF16)

    gate = jax.nn.sigmoid(jnp.dot(xn_ref[...], wg_ref[...], preferred_element_type=F32))
    proj = jnp.dot(pe_ref[...], wp_ref[...], preferred_element_type=F32)
    o_ref[...] = r_ref[...] + gate * proj


def ple_update(h, g, wg, pe, wp, *, tm, tn):
    m, k = h.shape
    n = wg.shape[1]
    kp = pe.shape[1]
    assert m % tm == 0 and n % tn == 0
    return pl.pallas_call(
        _ple_kernel,
        grid=(m // tm, n // tn),
        in_specs=[
            pl.BlockSpec((tm, k), lambda i, j: (i, 0)),
            pl.BlockSpec((1, k), lambda i, j: (0, 0)),
            pl.BlockSpec((k, tn), lambda i, j: (0, j)),
            pl.BlockSpec((tm, kp), lambda i, j: (i, 0)),
            pl.BlockSpec((kp, tn), lambda i, j: (0, j)),
            pl.BlockSpec((tm, tn), lambda i, j: (i, j)),
        ],
        out_specs=pl.BlockSpec((tm, tn), lambda i, j: (i, j)),
        out_shape=jax.ShapeDtypeStruct((m, n), F32),
        scratch_shapes=[pltpu.VMEM((tm, k), BF16)],
        compiler_params=_cparams(2),
        name="ple_update",
    )(h, g.reshape(1, k), wg, pe, wp, h)


def _rmsnorm_kernel(x_ref, g_ref, o_ref):
    o_ref[...] = _rms_rows(x_ref[...], g_ref[...])


def rmsnorm(x, g, *, tm):
    m, k = x.shape
    assert m % tm == 0
    return pl.pallas_call(
        _rmsnorm_kernel,
        grid=(m // tm,),
        in_specs=[pl.BlockSpec((tm, k), lambda i: (i, 0)), pl.BlockSpec((1, k), lambda i: (0, 0))],
        out_specs=pl.BlockSpec((tm, k), lambda i: (i, 0)),
        out_shape=jax.ShapeDtypeStruct((m, k), F32),
        compiler_params=_cparams(1),
        name="final_rmsnorm",
    )(x, g.reshape(1, k))


def _pool_kernel(u_ref, buf_ref, w_ref, scale_ref, z_ref, full_ref, *, tt, pos0):
    t = pl.program_id(1)

    @pl.when(t == 0)
    def _():
        full_ref[0:POOL_CARRY, :] = buf_ref[0]

    @pl.when(t != 0)
    def _():
        full_ref[0:POOL_CARRY, :] = full_ref[tt:tt + POOL_CARRY, :]

    full_ref[POOL_CARRY:POOL_CARRY + tt, :] = u_ref[...]
    gc = w_ref.shape[1]
    pos = pos0 + t * tt + lax.broadcasted_iota(jnp.int32, (tt, 1), 0)
    for g, win in enumerate(POOL_WINDOWS):
        cols = slice(g * gc, (g + 1) * gc)
        cur = full_ref[POOL_CARRY:POOL_CARRY + tt, cols]
        tot = cur
        for i in range(1, win):
            tot = tot + full_ref[POOL_CARRY - i:POOL_CARRY - i + tt, cols]
        cnt = jnp.minimum(pos + 1, win).astype(F32)
        d = tot / cnt - cur
        z = jnp.dot(d.astype(BF16), w_ref[g], preferred_element_type=F32)
        z_ref[:, cols] = (z * scale_ref[:, cols]).astype(z_ref.dtype)


def pool_mixer(proj, buf16, w, scale, *, bsz, t_len, tt, pos0):
    c = scale.shape[-1]
    nt = t_len // tt
    assert t_len % tt == 0 and (nt == 1 or tt >= POOL_CARRY)
    return pl.pallas_call(
        functools.partial(_pool_kernel, tt=tt, pos0=pos0),
        grid=(bsz, nt),
        in_specs=[
            pl.BlockSpec((tt, c), lambda b, t: (b * nt + t, 0)),
            pl.BlockSpec((1, POOL_CARRY, c), lambda b, t: (b, 0, 0)),
            pl.BlockSpec(w.shape, lambda b, t: (0, 0, 0)),
            pl.BlockSpec((1, c), lambda b, t: (0, 0)),
        ],
        out_specs=pl.BlockSpec((tt, c), lambda b, t: (b * nt + t, 0)),
        out_shape=jax.ShapeDtypeStruct((bsz * t_len, c), BF16),
        scratch_shapes=[pltpu.VMEM((POOL_CARRY + tt, c), F32)],
        compiler_params=_cparams(2),
        name="pool_mixer",
    )(proj, buf16, w, scale.reshape(1, c))


def _lambda_value(lam_ref, lam_init):
    v = lam_ref[...]
    s1 = jnp.sum(v[0:1] * v[1:2], axis=-1, keepdims=True)
    s2 = jnp.sum(v[2:3] * v[3:4], axis=-1, keepdims=True)
    return jnp.exp(s1) - jnp.exp(s2) + lam_init


def _head_slopes(head_idx_f32):
    return jnp.exp2((-8.0 / DA_HEADS) * (head_idx_f32 + 1.0))


def _subln(o, g, lam_init):
    ms = jnp.mean(o * o, axis=-1, keepdims=True)
    return o * lax.rsqrt(ms + EPS) * g * (1.0 - lam_init)


def _online_softmax_step(s, v_bf16, m_ref, l_ref, acc_ref, rows):
    m_old = m_ref[rows, :]
    m_new = jnp.maximum(m_old, jnp.max(s, axis=-1, keepdims=True))
    alpha = jnp.exp(m_old - m_new)
    p = jnp.exp(s - m_new)
    l_ref[rows, :] = alpha * l_ref[rows, :] + jnp.sum(p, axis=-1, keepdims=True)
    acc_ref[rows, :] = alpha * acc_ref[rows, :] + jnp.dot(
        p.astype(BF16), v_bf16, preferred_element_type=F32)
    m_ref[rows, :] = m_new


def _attn_prompt_kernel(q_ref, k_ref, v_ref, lam_ref, g_ref, o_ref, m_ref, l_ref, acc_ref,
                        *, tq, tk, lam_init):
    hd = DA_HEAD_DIM
    h = pl.program_id(1)
    qi = pl.program_id(2)
    slope = _head_slopes(jnp.full((1, 1), h, jnp.int32).astype(F32))
    q = (q_ref[...] * (hd ** -0.5)).astype(BF16)
    m_ref[...] = jnp.full(m_ref.shape, NEG_INF, F32)
    l_ref[...] = jnp.zeros(l_ref.shape, F32)
    acc_ref[...] = jnp.zeros(acc_ref.shape, F32)
    rel = (lax.broadcasted_iota(jnp.int32, (tq, tk), 0)
           - lax.broadcasted_iota(jnp.int32, (tq, tk), 1))

    def body(ki, carry):
        start = pl.multiple_of(ki * tk, tk)
        k = k_ref[pl.ds(start, tk), :].astype(BF16)
        v = v_ref[pl.ds(start, tk), :].astype(BF16)
        dist = (rel + (qi * tq - ki * tk)).astype(F32)
        bias = jnp.where(dist >= 0.0, -slope * dist, NEG_INF)
        for c in range(2):
            s = lax.dot_general(q[:, c * hd:(c + 1) * hd], k[:, c * hd:(c + 1) * hd],
                                (((1,), (1,)), ((), ())), preferred_element_type=F32) + bias
            _online_softmax_step(s, v, m_ref, l_ref, acc_ref, slice(c * tq, (c + 1) * tq))
        return carry

    lax.fori_loop(0, (qi * tq) // tk + 1, body, 0)
    lam = _lambda_value(lam_ref, lam_init)
    o = (acc_ref[0:tq, :] / l_ref[0:tq, :] - lam * (acc_ref[tq:2 * tq, :] / l_ref[tq:2 * tq, :]))
    o_ref[...] = _subln(o, g_ref[...], lam_init).astype(o_ref.dtype)


def attention_prompt(proj, lam_vecs, subln_g, *, bsz, t_len, q_col, k_col, v_col, tq, tk, lam_init):
    hw = 2 * DA_HEAD_DIM
    nq = t_len // tq
    assert t_len % tq == 0 and tq % tk == 0
    return pl.pallas_call(
        functools.partial(_attn_prompt_kernel, tq=tq, tk=tk, lam_init=lam_init),
        grid=(bsz, DA_HEADS, nq),
        in_specs=[
            pl.BlockSpec((tq, hw), lambda b, h, i: (b * nq + i, q_col + h)),
            pl.BlockSpec((t_len, hw), lambda b, h, i: (b, k_col + h)),
            pl.BlockSpec((t_len, hw), lambda b, h, i: (b, v_col + h)),
            pl.BlockSpec((4, DA_HEAD_DIM), lambda b, h, i: (0, 0)),
            pl.BlockSpec((1, hw), lambda b, h, i: (0, 0)),
        ],
        out_specs=pl.BlockSpec((tq, hw), lambda b, h, i: (b * nq + i, h)),
        out_shape=jax.ShapeDtypeStruct((bsz * t_len, DA_HEADS * hw), BF16),
        scratch_shapes=[pltpu.VMEM((2 * tq, 1), F32), pltpu.VMEM((2 * tq, 1), F32),
                        pltpu.VMEM((2 * tq, hw), F32)],
        compiler_params=_cparams(3),
        name="attention_prompt",
    )(proj, proj, proj, lam_vecs, subln_g.reshape(1, hw))


def _attn_sample_kernel(pt_ref, q_ref, *refs, n_pages, tpad, pos0, lam_init):
    del pt_ref
    k_refs = refs[:n_pages]
    v_refs = refs[n_pages:2 * n_pages]
    kn_ref, vn_ref, lam_ref, g_ref, o_ref, qall_ref, bias_ref, m_ref, l_ref, acc_ref = refs[2 * n_pages:]
    hd = DA_HEAD_DIM
    nh = DA_HEADS
    page = k_refs[0].shape[2]
    nrow = nh * 2 * tpad
    ncol = page * nh
    sidx = pl.program_id(1)
    sh_row = int(math.log2(2 * tpad))
    sh_col = int(math.log2(nh))
    assert 1 << sh_row == 2 * tpad and 1 << sh_col == nh

    def rel_and_match():
        row = lax.broadcasted_iota(jnp.int32, (nrow, ncol), 0)
        col = lax.broadcasted_iota(jnp.int32, (nrow, ncol), 1)
        rel = (jnp.right_shift(col, sh_col) - jnp.bitwise_and(row, tpad - 1)).astype(F32)
        match = jnp.right_shift(row, sh_row) == jnp.bitwise_and(col, nh - 1)
        return rel, match

    def row_slopes(shape):
        row = lax.broadcasted_iota(jnp.int32, shape, 0)
        return _head_slopes(jnp.right_shift(row, sh_row).astype(F32))

    @pl.when(sidx == 0)
    def _():
        q = q_ref[...] * (hd ** -0.5)
        lane = lax.broadcasted_iota(jnp.int32, (tpad, 2 * hd), 1)
        for h in range(nh):
            qh = q[:, h * 2 * hd:(h + 1) * 2 * hd]
            qall_ref[h * 2 * tpad:h * 2 * tpad + tpad, :] = jnp.where(lane < hd, qh, 0.0).astype(BF16)
            qall_ref[h * 2 * tpad + tpad:(h + 1) * 2 * tpad, :] = jnp.where(lane >= hd, qh, 0.0).astype(BF16)
        m_ref[...] = jnp.full(m_ref.shape, NEG_INF, F32)
        l_ref[...] = jnp.zeros(l_ref.shape, F32)
        acc_ref[...] = jnp.zeros(acc_ref.shape, F32)
        rel, match = rel_and_match()
        bias_ref[...] = jnp.where(match, row_slopes((nrow, ncol)) * rel, NEG_INF)

    def scores(k_page):
        kk = k_page.reshape(ncol, 2 * hd).astype(BF16)
        return lax.dot_general(qall_ref[...], kk, (((1,), (1,)), ((), ())), preferred_element_type=F32)

    slope_col = row_slopes((nrow, 1))
    for i in range(n_pages):
        base = ((sidx * n_pages + i) * page - pos0).astype(F32)
        s = scores(k_refs[i][0, 0]) + bias_ref[...] + slope_col * base
        vv = v_refs[i][0, 0].reshape(ncol, 2 * hd).astype(BF16)
        _online_softmax_step(s, vv, m_ref, l_ref, acc_ref, slice(0, nrow))

    @pl.when(sidx == pl.num_programs(1) - 1)
    def _():
        rel, _ = rel_and_match()
        s = scores(kn_ref[0]) + jnp.where(rel <= 0.0, bias_ref[...], NEG_INF)
        vv = vn_ref[0].reshape(ncol, 2 * hd).astype(BF16)
        _online_softmax_step(s, vv, m_ref, l_ref, acc_ref, slice(0, nrow))
        lam = _lambda_value(lam_ref, lam_init)
        for h in range(nh):
            r0 = h * 2 * tpad
            a0 = acc_ref[r0:r0 + tpad, :] / l_ref[r0:r0 + tpad, :]
            a1 = acc_ref[r0 + tpad:r0 + 2 * tpad, :] / l_ref[r0 + tpad:r0 + 2 * tpad, :]
            o = _subln(a0 - lam * a1, g_ref[...], lam_init)
            o_ref[:, h * 2 * hd:(h + 1) * 2 * hd] = o.astype(o_ref.dtype)


def attention_sample(proj, cache_k, cache_v, layer, page_table, k_new, v_new, lam_vecs, subln_g,
                     *, bsz, tpad, q_col, n_pages, lam_init):
    hw = 2 * DA_HEAD_DIM
    qw = DA_HEADS * hw
    page = cache_k.shape[2]
    n_past_pages = page_table.shape[1]
    assert n_past_pages % n_pages == 0
    pos0 = n_past_pages * page
    nrow = DA_HEADS * 2 * tpad
    page_block = (1, 1, page, DA_HEADS, hw)

    def page_spec(i):
        return pl.BlockSpec(page_block, lambda b, s, pt: (layer, pt[b, s * n_pages + i], 0, 0, 0))

    new_spec = pl.BlockSpec((1, page, DA_HEADS, hw), lambda b, s, pt: (b, 0, 0, 0))
    grid_spec = pltpu.PrefetchScalarGridSpec(
        num_scalar_prefetch=1,
        grid=(bsz, n_past_pages // n_pages),
        in_specs=[pl.BlockSpec((tpad, qw), lambda b, s, pt: (b, q_col))]
        + [page_spec(i) for i in range(n_pages)] * 2
        + [new_spec, new_spec,
           pl.BlockSpec((4, DA_HEAD_DIM), lambda b, s, pt: (0, 0)),
           pl.BlockSpec((1, hw), lambda b, s, pt: (0, 0))],
        out_specs=pl.BlockSpec((tpad, qw), lambda b, s, pt: (b, 0)),
        scratch_shapes=[pltpu.VMEM((nrow, hw), BF16), pltpu.VMEM((nrow, page * DA_HEADS), F32),
                        pltpu.VMEM((nrow, 1), F32), pltpu.VMEM((nrow, 1), F32),
                        pltpu.VMEM((nrow, hw), F32)],
    )
    return pl.pallas_call(
        functools.partial(_attn_sample_kernel, n_pages=n_pages, tpad=tpad, pos0=pos0, lam_init=lam_init),
        grid_spec=grid_spec,
        out_shape=jax.ShapeDtypeStruct((bsz * tpad, qw), BF16),
        compiler_params=_cparams(2),
        name="attention_sample",
    )(page_table, proj, *([cache_k] * n_pages), *([cache_v] * n_pages), k_new, v_new,
      lam_vecs, subln_g.reshape(1, hw))


LANES = 128


def _conformer_kernel(a_ref, gate_ref, buf_ref, dww_ref, dwb_ref, lng_ref, lnb_ref, pww_ref, pwb_ref,
                      z_ref, glu_ref, full_ref, cv_ref, *, tt):
    t = pl.program_id(1)

    @pl.when(t == 0)
    def _():
        full_ref[0:CONV_CARRY, :] = buf_ref[0]

    @pl.when(t != 0)
    def _():
        full_ref[0:CONV_CARRY, :] = full_ref[tt:tt + CONV_CARRY, :]

    glu = a_ref[...] * jax.nn.sigmoid(gate_ref[...])
    full_ref[CONV_CARRY:CONV_CARRY + tt, :] = glu
    glu_ref[...] = glu
    first = CONV_CARRY - (CONV_K - 1)

    def lane_chunk(c, carry):
        col = pl.ds(pl.multiple_of(c * LANES, LANES), LANES)
        acc = jnp.zeros((tt, LANES), F32)
        for k in range(CONV_K):
            acc = acc + dww_ref[k:k + 1, col] * full_ref[first + k:first + k + tt, col]
        cv_ref[:, col] = acc + dwb_ref[:, col]
        return carry

    lax.fori_loop(0, cv_ref.shape[1] // LANES, lane_chunk, 0)
    y = cv_ref[...]
    mu = jnp.mean(y, axis=-1, keepdims=True)
    yc = y - mu
    var = jnp.mean(yc * yc, axis=-1, keepdims=True)
    y = yc * lax.rsqrt(var + EPS) * lng_ref[...] + lnb_ref[...]
    y = y * jax.nn.sigmoid(y)
    z = jnp.dot(y.astype(BF16), pww_ref[...], preferred_element_type=F32) + pwb_ref[...]
    z_ref[...] = z.astype(z_ref.dtype)


def conformer_mixer(proj, buf32, dw_w, dw_b, ln_g, ln_b, pw_w, pw_b, *, bsz, t_len, tt):
    c = dw_b.shape[-1]
    nt = t_len // tt
    assert t_len % tt == 0 and (nt == 1 or tt >= CONV_CARRY)
    vec = lambda: pl.BlockSpec((1, c), lambda b, t: (0, 0))
    rows = lambda j: pl.BlockSpec((tt, c), lambda b, t: (b * nt + t, j))
    return pl.pallas_call(
        functools.partial(_conformer_kernel, tt=tt),
        grid=(bsz, nt),
        in_specs=[
            rows(0), rows(1),
            pl.BlockSpec((1, CONV_CARRY, c), lambda b, t: (b, 0, 0)),
            pl.BlockSpec((CONV_K, c), lambda b, t: (0, 0)),
            vec(), vec(), vec(),
            pl.BlockSpec((c, c), lambda b, t: (0, 0)),
            vec(),
        ],
        out_specs=[rows(0), rows(0)],
        out_shape=[jax.ShapeDtypeStruct((bsz * t_len, c), BF16),
                   jax.ShapeDtypeStruct((bsz * t_len, c), F32)],
        scratch_shapes=[pltpu.VMEM((CONV_CARRY + tt, c), F32), pltpu.VMEM((tt, c), F32)],
        compiler_params=_cparams(2),
        name="conformer_mixer",
    )(proj, proj, buf32, dw_w, dw_b.reshape(1, c), ln_g.reshape(1, c), ln_b.reshape(1, c),
      pw_w, pw_b.reshape(1, c))


def _dot_f32(a, b):
    return jnp.dot(a, b, preferred_element_type=F32, precision=lax.Precision.HIGHEST)


def _ssd_kernel(z_ref, x_ref, b_ref, c_ref, dt_ref, buf_ref, cw_ref, cb_ref, dtb_ref, alog_ref,
                dexp_ref, ng_ref, e_ref, et_ref, h0_ref, y_ref, hn_ref, full_ref, st_ref,
                *, t_valid):
    t = pl.program_id(1)
    ln = x_ref.shape[0]
    inner = x_ref.shape[1]
    gn = b_ref.shape[1]
    ns = SSM_STATE
    gw = inner // SSM_GROUPS
    hpg = gw // SSM_HEAD_DIM

    @pl.when(t == 0)
    def _():
        full_ref[0:SSM_CARRY, :] = buf_ref[0]
        st_ref[...] = h0_ref[0]

    @pl.when(t != 0)
    def _():
        full_ref[0:SSM_CARRY, :] = full_ref[ln:ln + SSM_CARRY, :]

    full_ref[SSM_CARRY:SSM_CARRY + ln, 0:inner] = x_ref[...]
    full_ref[SSM_CARRY:SSM_CARRY + ln, inner:inner + gn] = b_ref[...]
    full_ref[SSM_CARRY:SSM_CARRY + ln, inner + gn:inner + 2 * gn] = c_ref[...]
    first = SSM_CARRY - (SSM_CONV_K - 1)
    xc = cb_ref[...] + cw_ref[0:1, :] * full_ref[first:first + ln, :]
    for k in range(1, SSM_CONV_K):
        xc = xc + cw_ref[k:k + 1, :] * full_ref[first + k:first + k + ln, :]
    xc = xc * jax.nn.sigmoid(xc)
    xs = xc[:, 0:inner]
    bm = xc[:, inner:inner + gn].astype(BF16)
    cm = xc[:, inner + gn:inner + 2 * gn].astype(BF16)

    dt_in = dt_ref[...] + dtb_ref[...]
    dt = jnp.maximum(dt_in, 0.0) + jnp.log1p(jnp.exp(-jnp.abs(dt_in)))
    row = lax.broadcasted_iota(jnp.int32, (ln, ln), 0)
    col = lax.broadcasted_iota(jnp.int32, (ln, ln), 1)
    if t_valid < ln:
        dt = jnp.where(lax.broadcasted_iota(jnp.int32, dt.shape, 0) < t_valid, dt, 0.0)
    ad = dt * (-jnp.exp(alog_ref[...]))
    causal = row >= col
    a_cs = _dot_f32(causal.astype(F32), ad)
    a_cs_t = a_cs.T
    a_last = a_cs[ln - 1:ln, :]
    per_head = jnp.concatenate([dt, jnp.exp(a_last - a_cs), jnp.exp(a_cs)], axis=0)
    per_chan = _dot_f32(per_head, e_ref[...])
    xd = xs * per_chan[0:ln]
    xdw = (xd * per_chan[ln:2 * ln]).astype(BF16)
    xd = xd.astype(BF16)
    exp_acs = per_chan[2 * ln:3 * ln]
    last_t = jnp.broadcast_to(jnp.exp(a_cs_t[:, ln - 1:ln]), a_cs_t.shape)
    st_decay = _dot_f32(et_ref[...], last_t)

    lane_head = lax.broadcasted_iota(jnp.int32, (ln, gw), 1) // SSM_HEAD_DIM
    for g in range(SSM_GROUPS):
        cols = slice(g * gw, (g + 1) * gw)
        scols = slice(g * ns, (g + 1) * ns)
        scores = lax.dot_general(cm[:, scols], bm[:, scols], (((1,), (1,)), ((), ())),
                                 preferred_element_type=F32)
        lhs, rhs = [], []
        for r in range(hpg):
            h = g * hpg + r
            seg = a_cs[:, h:h + 1] - a_cs_t[h:h + 1, :]
            lhs.append((scores * jnp.where(causal, jnp.exp(seg), 0.0)).astype(BF16))
            rhs.append(jnp.where(lane_head == r, xd[:, cols], jnp.zeros_like(xd[:, cols])))
        y = jnp.dot(jnp.concatenate(lhs, axis=1), jnp.concatenate(rhs, axis=0),
                    preferred_element_type=F32)
        h_prev = st_ref[cols, :]
        y = y + exp_acs[:, cols] * lax.dot_general(
            cm[:, scols], h_prev.astype(BF16), (((1,), (1,)), ((), ())), preferred_element_type=F32)
        new_states = lax.dot_general(xdw[:, cols], bm[:, scols], (((0,), (0,)), ((), ())),
                                     preferred_element_type=F32)
        st_ref[cols, :] = h_prev * st_decay[cols, :] + new_states
        y = y + dexp_ref[:, cols] * xs[:, cols]
        zg = z_ref[:, cols]
        y = y * (zg * jax.nn.sigmoid(zg))
        y_ref[:, cols] = (_rms_rows(y, ng_ref[:, cols])).astype(y_ref.dtype)

    @pl.when(t == pl.num_programs(1) - 1)
    def _():
        hn_ref[0] = st_ref[...]


def ssd_mixer(proj, dt_raw, buf8, h0, conv_w, conv_b, dt_bias, a_log, d_skip, norm_g,
              *, bsz, t_len, t_valid, z_col, x_col, b_col, c_col):
    inner = norm_g.shape[-1]
    heads = inner // SSM_HEAD_DIM
    gn = SSM_GROUPS * SSM_STATE
    xbc = inner + 2 * gn
    ln = SSM_CHUNK
    nc = t_len // ln
    assert t_len % ln == 0 and heads <= LANES

    def pad_heads(v):
        return jnp.pad(v.astype(F32), (0, LANES - heads)).reshape(1, LANES)

    expand = (jnp.arange(LANES)[:, None] == (jnp.arange(inner) // SSM_HEAD_DIM)[None, :]).astype(F32)
    d_exp = jnp.repeat(d_skip.astype(F32), SSM_HEAD_DIM).reshape(1, inner)
    const = lambda shape: pl.BlockSpec(shape, lambda b, t: (0,) * len(shape))
    rows = lambda w, j: pl.BlockSpec((ln, w), lambda b, t: (b * nc + t, j))
    return pl.pallas_call(
        functools.partial(_ssd_kernel, t_valid=t_valid),
        grid=(bsz, nc),
        in_specs=[
            rows(inner, z_col), rows(inner, x_col), rows(gn, b_col), rows(gn, c_col),
            rows(LANES, 0),
            pl.BlockSpec((1, SSM_CARRY, xbc), lambda b, t: (b, 0, 0)),
            const((SSM_CONV_K, xbc)), const((1, xbc)),
            const((1, LANES)), const((1, LANES)),
            const((1, inner)), const((1, inner)),
            const((LANES, inner)), const((inner, LANES)),
            pl.BlockSpec((1, inner, SSM_STATE), lambda b, t: (b, 0, 0)),
        ],
        out_specs=[rows(inner, 0), pl.BlockSpec((1, inner, SSM_STATE), lambda b, t: (b, 0, 0))],
        out_shape=[jax.ShapeDtypeStruct((bsz * t_len, inner), BF16),
                   jax.ShapeDtypeStruct((bsz, inner, SSM_STATE), F32)],
        scratch_shapes=[pltpu.VMEM((SSM_CARRY + ln, xbc), F32), pltpu.VMEM((inner, SSM_STATE), F32)],
        compiler_params=_cparams(2),
        name="ssd_mixer",
    )(proj, proj, proj, proj, dt_raw, buf8, conv_w, conv_b.reshape(1, xbc), pad_heads(dt_bias),
      pad_heads(a_log), d_exp, norm_g.reshape(1, inner), expand, expand.T, h0)


def _tail_rows(buf, new_rows, n):
    return jnp.concatenate([buf, new_rows], axis=1)[:, -n:]


def _pad_front(buf, rows):
    return jnp.pad(buf, ((0, 0), (rows - buf.shape[1], 0), (0, 0)))


def _trunk(x, pemb, wts, states, *, bsz, t_len, t_valid, tm, tt, paged):
    depth = pemb.shape[0]
    d_model = x.shape[1]
    pool_w = d_model // 2
    da_w = DA_HEADS * 2 * DA_HEAD_DIM
    hw = 2 * DA_HEAD_DIM
    conv_w = d_model // 2
    inner = d_model // 2
    gn = SSM_GROUPS * SSM_STATE
    xbc_w = inner + 2 * gn
    tn = 1024
    tm2 = min(2 * tm, bsz * t_len)
    pos0 = 0 if paged is None else paged[2].shape[1] * paged[0].shape[2]
    h = x
    outs = dict(k=[], v=[], pool=[], conf=[], mconv=[], ssm=[])

    def seq_rows(a, width_slice):
        return a[:, width_slice].reshape(bsz, t_len, -1)[:, :t_valid]

    for i in range(depth):
        j = i // 2
        if i % 2 == 0:
            proj = norm_matmul(h, wts["g_mix"][i], wts["w_in_even"][j], tm=tm, tn=tn)
            k_seq = seq_rows(proj, slice(pool_w + da_w, pool_w + 2 * da_w))
            v_seq = seq_rows(proj, slice(pool_w + 2 * da_w, pool_w + 3 * da_w))
            outs["k"].append(k_seq.reshape(bsz, t_valid, DA_HEADS, hw))
            outs["v"].append(v_seq.reshape(bsz, t_valid, DA_HEADS, hw))
            pool_buf = states["pool"][j]
            outs["pool"].append(_tail_rows(pool_buf, seq_rows(proj, slice(0, pool_w)), POOL_BUF))
            z_pool = pool_mixer(proj, _pad_front(pool_buf, POOL_CARRY), wts["pool_w"][j],
                                wts["pool_scale"][j], bsz=bsz, t_len=t_len, tt=tt, pos0=pos0)
            lam_init = 0.8 - 0.6 * math.exp(-0.3 * i)
            lam_vecs = jnp.stack([wts["lambda_q1"][j], wts["lambda_k1"][j],
                                  wts["lambda_q2"][j], wts["lambda_k2"][j]]).astype(F32)
            q_col = pool_w // hw
            if paged is None:
                attn = attention_prompt(
                    proj, lam_vecs, wts["subln_g"][j], bsz=bsz, t_len=t_len, q_col=q_col,
                    k_col=q_col + DA_HEADS, v_col=q_col + 2 * DA_HEADS, tq=256, tk=256,
                    lam_init=lam_init)
            else:
                cache_k, cache_v, page_table = paged
                page = cache_k.shape[2]

                def new_page(rows):
                    rows = rows.reshape(bsz, t_valid, DA_HEADS, hw)
                    return jnp.pad(rows, ((0, 0), (0, page - t_valid), (0, 0), (0, 0)))

                attn = attention_sample(
                    proj, cache_k, cache_v, j, page_table, new_page(k_seq), new_page(v_seq),
                    lam_vecs, wts["subln_g"][j], bsz=bsz, tpad=t_len, q_col=pool_w // da_w,
                    n_pages=4, lam_init=lam_init)
            h = matmul2_residual(z_pool, attn, wts["w_out_even"][j], h, tm=tm2, tn=tn // 2)
        else:
            w_in = wts["w_in_odd"][j]
            proj = norm_matmul(h, wts["g_mix"][i], w_in, tm=tm, tn=w_in.shape[1] // 9)
            conf_buf = states["conf"][j]
            zc, glu = conformer_mixer(
                proj, _pad_front(conf_buf, CONV_CARRY), wts["conf_dw_w"][j], wts["conf_dw_b"][j],
                wts["conf_ln_g"][j], wts["conf_ln_b"][j], wts["conf_pw_w"][j], wts["conf_pw_b"][j],
                bsz=bsz, t_len=t_len, tt=tt)
            outs["conf"].append(_tail_rows(conf_buf, glu.reshape(bsz, t_len, conv_w)[:, :t_valid],
                                           CONV_K - 1))
            mconv_buf = states["mconv"][j]
            x0 = 2 * conv_w + inner
            outs["mconv"].append(_tail_rows(mconv_buf, seq_rows(proj, slice(x0, x0 + xbc_w)),
                                            SSM_CONV_K - 1))
            t_ssd = -(-t_len // SSM_CHUNK) * SSM_CHUNK
            proj_ssd = proj
            if t_ssd != t_len:
                proj_ssd = jnp.pad(proj.reshape(bsz, t_len, -1),
                                   ((0, 0), (0, t_ssd - t_len), (0, 0))).reshape(bsz * t_ssd, -1)
            dt_raw = proj_ssd[:, x0 + xbc_w:x0 + xbc_w + LANES]
            y, h_new = ssd_mixer(
                proj_ssd, dt_raw, _pad_front(mconv_buf, SSM_CARRY), states["ssm"][j],
                wts["ssm_conv_w"][j], wts["ssm_conv_b"][j], wts["ssm_dt_bias"][j],
                wts["ssm_A_log"][j], wts["ssm_D"][j], wts["ssm_norm_g"][j],
                bsz=bsz, t_len=t_ssd, t_valid=min(t_valid, SSM_CHUNK),
                z_col=2 * conv_w // inner, x_col=x0 // inner, b_col=(x0 + inner) // gn,
                c_col=(x0 + inner + gn) // gn)
            if t_ssd != t_len:
                y = y.reshape(bsz, t_ssd, inner)[:, :t_len].reshape(bsz * t_len, inner)
            outs["ssm"].append(h_new)
            h = matmul2_residual(zc, y, wts["w_out_odd"][j], h, tm=tm2, tn=tn // 2)
        hidden = norm_matmul(h, wts["g_mlp"][i], wts["w_up"][i], tm=tm, tn=tn, act="relu2",
                             out_dtype=BF16)
        h = matmul_k_residual(hidden, wts["w_down"][i], h, tm=tm2, tn=tn, tk=2048)
        h = ple_update(h, wts["g_ple"][i], wts["w_ple_gate"][i], pemb[i], wts["w_ple_proj"][i],
                       tm=tm, tn=tn // 2)
    y = rmsnorm(h, wts["g_final"], tm=min(256, bsz * t_len))
    return y, outs


def kernel(x_prompt, x_sample, p_prompt, p_sample, cache_k, cache_v, page_table, state_pool, state_conf_conv, state_ssm_conv, state_ssm, g_mix, g_mlp, g_ple, g_final, w_in_even, pool_w, pool_scale, lambda_q1, lambda_k1, lambda_q2, lambda_k2, subln_g, w_out_even, w_in_odd, conf_dw_w, conf_dw_b, conf_ln_g, conf_ln_b, conf_pw_w, conf_pw_b, ssm_conv_w, ssm_conv_b, ssm_dt_bias, ssm_A_log, ssm_D, ssm_norm_g, w_out_odd, w_up, w_down, w_ple_proj, w_ple_gate):
    bp, tp, d_model = x_prompt.shape
    bs, ts, _ = x_sample.shape
    depth = p_prompt.shape[0]
    n_even, n_odd = (depth + 1) // 2, depth // 2
    inner = ssm_norm_g.shape[-1]
    pad_cols = -w_in_odd.shape[-1] % LANES
    wts = dict(
        g_mix=g_mix, g_mlp=g_mlp, g_ple=g_ple, g_final=g_final,
        w_in_even=w_in_even.astype(BF16), pool_w=pool_w.astype(BF16), pool_scale=pool_scale,
        lambda_q1=lambda_q1, lambda_k1=lambda_k1, lambda_q2=lambda_q2, lambda_k2=lambda_k2,
        subln_g=subln_g, w_out_even=w_out_even.astype(BF16),
        w_in_odd=jnp.pad(w_in_odd.astype(BF16), ((0, 0), (0, 0), (0, pad_cols))),
        conf_dw_w=conf_dw_w, conf_dw_b=conf_dw_b, conf_ln_g=conf_ln_g, conf_ln_b=conf_ln_b,
        conf_pw_w=conf_pw_w.astype(BF16), conf_pw_b=conf_pw_b, ssm_conv_w=ssm_conv_w,
        ssm_conv_b=ssm_conv_b, ssm_dt_bias=ssm_dt_bias, ssm_A_log=ssm_A_log, ssm_D=ssm_D,
        ssm_norm_g=ssm_norm_g, w_out_odd=w_out_odd.astype(BF16), w_up=w_up.astype(BF16),
        w_down=w_down.astype(BF16), w_ple_proj=w_ple_proj.astype(BF16),
        w_ple_gate=w_ple_gate.astype(BF16))
    dtype = x_prompt.dtype

    zero_states = dict(
        pool=jnp.zeros((n_even, bp, POOL_BUF, pool_scale.shape[-1]), dtype),
        conf=jnp.zeros((n_odd, bp, CONV_K - 1, conf_dw_b.shape[-1]), dtype),
        mconv=jnp.zeros((n_odd, bp, SSM_CONV_K - 1, ssm_conv_b.shape[-1]), dtype),
        ssm=jnp.zeros((n_odd, bp, inner, SSM_STATE), F32))
    y_p, o_p = _trunk(
        x_prompt.reshape(bp * tp, d_model), p_prompt.reshape(depth, bp * tp, -1).astype(BF16), wts,
        zero_states, bsz=bp, t_len=tp, t_valid=tp, tm=512, tt=256, paged=None)

    t_pad = SAMPLE_T_PAD
    pad_t = lambda a, axis: jnp.pad(a, [(0, t_pad - ts) if d == axis else (0, 0) for d in range(a.ndim)])
    sample_states = dict(pool=state_pool, conf=state_conf_conv, mconv=state_ssm_conv,
                         ssm=state_ssm.astype(F32).reshape(n_odd, bs, inner, SSM_STATE))
    y_s, o_s = _trunk(
        pad_t(x_sample, 1).reshape(bs * t_pad, d_model),
        pad_t(p_sample, 2).reshape(depth, bs * t_pad, -1).astype(BF16), wts, sample_states,
        bsz=bs, t_len=t_pad, t_valid=ts, tm=bs * t_pad, tt=t_pad,
        paged=(cache_k, cache_v, page_table))

    def ssm_out(states, bsz):
        return jnp.stack(states).reshape(n_odd, bsz, inner // SSM_HEAD_DIM, SSM_HEAD_DIM,
                                         SSM_STATE).astype(state_ssm.dtype)

    return (y_p.reshape(bp, tp, d_model), y_s.reshape(bs, t_pad, d_model)[:, :ts],
            jnp.stack(o_p["k"]), jnp.stack(o_p["v"]), jnp.stack(o_s["k"]), jnp.stack(o_s["v"]),
            jnp.stack(o_p["pool"]), jnp.stack(o_s["pool"]),
            jnp.stack(o_p["conf"]), jnp.stack(o_s["conf"]),
            jnp.stack(o_p["mconv"]), jnp.stack(o_s["mconv"]),
            ssm_out(o_p["ssm"], bp), ssm_out(o_s["ssm"], bs))
```

```python
import functools
import math

import jax
import jax.numpy as jnp
from jax import lax
from jax.experimental import pallas as pl
from jax.experimental.pallas import tpu as pltpu

F32 = jnp.float32
BF16 = jnp.bfloat16
EPS = 1e-6
NEG_INF = -1e30

V7X_VMEM_BYTES = 64 * 1024 * 1024
VMEM_LIMIT = V7X_VMEM_BYTES - 4 * 1024 * 1024

LANES = 128
SUBLANES = 8
POOL_WINDOWS = (2, 4, 8, 16)
POOL_BUF = max(POOL_WINDOWS) - 1
POOL_CARRY = 16
CONV_K = 31
CONV_CARRY = 32
SSM_CONV_K = 4
SSM_CARRY = 8
SSM_CHUNK = 128
DA_HEADS = 8
DA_HEAD_DIM = 128
SSM_HEAD_DIM = 64
SSM_GROUPS = 8
SSM_STATE = 128
SAMPLE_T_PAD = 16


def _cparams(n_axes):
    return pltpu.CompilerParams(
        dimension_semantics=("arbitrary",) * n_axes, vmem_limit_bytes=VMEM_LIMIT)


def _rms_rows(x, g):
    ms = jnp.mean(x * x, axis=-1, keepdims=True)
    return x * lax.rsqrt(ms + EPS) * g


def _sigmoid(x):
    return 0.5 * jnp.tanh(0.5 * x) + 0.5


def _norm_mm_kernel(x_ref, g_ref, w_ref, o_ref, xn_ref, *, act):
    @pl.when(pl.program_id(1) == 0)
    def _():
        xn_ref[...] = _rms_rows(x_ref[...], g_ref[...]).astype(BF16)

    acc = jnp.dot(xn_ref[...], w_ref[...], preferred_element_type=F32)
    if act == "relu2":
        acc = jnp.square(jnp.maximum(acc, 0.0))
    o_ref[...] = acc.astype(o_ref.dtype)


def norm_matmul(x, g, w, layer, *, tm, tn, n_cols=None, act=None, out_dtype=F32):
    m, k = x.shape
    n = w.shape[2] if n_cols is None else n_cols
    assert m % tm == 0 and n % tn == 0
    return pl.pallas_call(
        functools.partial(_norm_mm_kernel, act=act),
        grid=(m // tm, n // tn),
        in_specs=[
            pl.BlockSpec((tm, k), lambda i, j: (i, 0)),
            pl.BlockSpec((1, k), lambda i, j: (0, 0)),
            pl.BlockSpec((None, k, tn), lambda i, j: (layer, 0, j)),
        ],
        out_specs=pl.BlockSpec((tm, tn), lambda i, j: (i, j)),
        out_shape=jax.ShapeDtypeStruct((m, n), out_dtype),
        scratch_shapes=[pltpu.VMEM((tm, k), BF16)],
        compiler_params=_cparams(2),
        name="norm_matmul",
    )(x, g.reshape(1, k), w)


def _mm2_res_kernel(a1_ref, a2_ref, w_ref, r_ref, o_ref):
    k1 = a1_ref.shape[1]
    acc = jnp.dot(a1_ref[...], w_ref[:k1, :], preferred_element_type=F32)
    acc += jnp.dot(a2_ref[...], w_ref[k1:, :], preferred_element_type=F32)
    o_ref[...] = r_ref[...] + acc


def matmul2_residual(a1, a2, w, layer, res, *, tm, tn):
    m, k1 = a1.shape
    k2 = a2.shape[1]
    n = w.shape[2]
    assert m % tm == 0 and n % tn == 0 and w.shape[1] == k1 + k2
    return pl.pallas_call(
        _mm2_res_kernel,
        grid=(m // tm, n // tn),
        in_specs=[
            pl.BlockSpec((tm, k1), lambda i, j: (i, 0)),
            pl.BlockSpec((tm, k2), lambda i, j: (i, 0)),
            pl.BlockSpec((None, k1 + k2, tn), lambda i, j: (layer, 0, j)),
            pl.BlockSpec((tm, tn), lambda i, j: (i, j)),
        ],
        out_specs=pl.BlockSpec((tm, tn), lambda i, j: (i, j)),
        out_shape=jax.ShapeDtypeStruct((m, n), F32),
        compiler_params=_cparams(2),
        name="matmul2_residual",
    )(a1, a2, w, res)


def _mmk_res_kernel(a_ref, w_ref, r_ref, o_ref):
    @pl.when(pl.program_id(2) == 0)
    def _():
        o_ref[...] = r_ref[...]

    o_ref[...] += jnp.dot(a_ref[...], w_ref[...], preferred_element_type=F32)


def matmul_k_residual(a, w, layer, res, *, tm, tn, tk):
    m, k = a.shape
    n = w.shape[2]
    assert m % tm == 0 and n % tn == 0 and k % tk == 0
    return pl.pallas_call(
        _mmk_res_kernel,
        grid=(m // tm, n // tn, k // tk),
        in_specs=[
            pl.BlockSpec((tm, tk), lambda i, j, l: (i, l)),
            pl.BlockSpec((None, tk, tn), lambda i, j, l: (layer, l, j)),
            pl.BlockSpec((tm, tn), lambda i, j, l: (i, j)),
        ],
        out_specs=pl.BlockSpec((tm, tn), lambda i, j, l: (i, j)),
        out_shape=jax.ShapeDtypeStruct((m, n), F32),
        compiler_params=_cparams(3),
        name="matmul_k_residual",
    )(a, w, res)


def _ple_kernel(x_ref, g_ref, wg_ref, pe_ref, wp_ref, r_ref, o_ref, xn_ref):
    @pl.when(pl.program_id(1) == 0)
    def _():
        xn_ref[...] = _rms_rows(x_ref[...], g_ref[...]).astype(BF16)

    gate = jax.nn.sigmoid(jnp.dot(xn_ref[...], wg_ref[...], preferred_element_type=F32))
    proj = jnp.dot(pe_ref[...], wp_ref[...], preferred_element_type=F32)
    o_ref[...] = r_ref[...] + gate * proj


def ple_update(h, g, wg, pe, wp, layer, *, tm, tn):
    m, k = h.shape
    n = wg.shape[2]
    kp = pe.shape[2]
    assert m % tm == 0 and n % tn == 0
    return pl.pallas_call(
        _ple_kernel,
        grid=(m // tm, n // tn),
        in_specs=[
            pl.BlockSpec((tm, k), lambda i, j: (i, 0)),
            pl.BlockSpec((1, k), lambda i, j: (0, 0)),
            pl.BlockSpec((None, k, tn), lambda i, j: (layer, 0, j)),
            pl.BlockSpec((None, tm, kp), lambda i, j: (layer, i, 0)),
            pl.BlockSpec((None, kp, tn), lambda i, j: (layer, 0, j)),
            pl.BlockSpec((tm, tn), lambda i, j: (i, j)),
        ],
        out_specs=pl.BlockSpec((tm, tn), lambda i, j: (i, j)),
        out_shape=jax.ShapeDtypeStruct((m, n), F32),
        scratch_shapes=[pltpu.VMEM((tm, k), BF16)],
        compiler_params=_cparams(2),
        name="ple_update",
    )(h, g.reshape(1, k), wg, pe, wp, h)


def _rmsnorm_kernel(x_ref, g_ref, o_ref):
    o_ref[...] = _rms_rows(x_ref[...], g_ref[...])


def rmsnorm(x, g, *, tm):
    m, k = x.shape
    assert m % tm == 0
    return pl.pallas_call(
        _rmsnorm_kernel,
        grid=(m // tm,),
        in_specs=[pl.BlockSpec((tm, k), lambda i: (i, 0)), pl.BlockSpec((1, k), lambda i: (0, 0))],
        out_specs=pl.BlockSpec((tm, k), lambda i: (i, 0)),
        out_shape=jax.ShapeDtypeStruct((m, k), F32),
        compiler_params=_cparams(1),
        name="final_rmsnorm",
    )(x, g.reshape(1, k))


def _pool_kernel(u_ref, buf_ref, w_ref, scale_ref, z_ref, full_ref, *, tt, pos0):
    t = pl.program_id(1)

    @pl.when(t == 0)
    def _():
        full_ref[0:POOL_CARRY, :] = buf_ref[0]

    @pl.when(t != 0)
    def _():
        full_ref[0:POOL_CARRY, :] = full_ref[tt:tt + POOL_CARRY, :]

    full_ref[POOL_CARRY:POOL_CARRY + tt, :] = u_ref[...]
    gc = w_ref.shape[1]
    pos = pos0 + t * tt + lax.broadcasted_iota(jnp.int32, (tt, 1), 0)
    for g, win in enumerate(POOL_WINDOWS):
        cols = slice(g * gc, (g + 1) * gc)
        cur = full_ref[POOL_CARRY:POOL_CARRY + tt, cols]
        tot = cur
        for i in range(1, win):
            tot = tot + full_ref[POOL_CARRY - i:POOL_CARRY - i + tt, cols]
        cnt = jnp.minimum(pos + 1, win).astype(F32)
        d = tot / cnt - cur
        z = jnp.dot(d.astype(BF16), w_ref[g], preferred_element_type=F32)
        z_ref[:, cols] = (z * scale_ref[:, cols]).astype(z_ref.dtype)


def pool_mixer(proj, buf16, w, layer, scale, *, bsz, t_len, tt, pos0):
    c = scale.shape[-1]
    nt = t_len // tt
    assert t_len % tt == 0 and (nt == 1 or tt >= POOL_CARRY)
    return pl.pallas_call(
        functools.partial(_pool_kernel, tt=tt, pos0=pos0),
        grid=(bsz, nt),
        in_specs=[
            pl.BlockSpec((tt, c), lambda b, t: (b * nt + t, 0)),
            pl.BlockSpec((1, POOL_CARRY, c), lambda b, t: (b, 0, 0)),
            pl.BlockSpec((None,) + w.shape[1:], lambda b, t: (layer, 0, 0, 0)),
            pl.BlockSpec((1, c), lambda b, t: (0, 0)),
        ],
        out_specs=pl.BlockSpec((tt, c), lambda b, t: (b * nt + t, 0)),
        out_shape=jax.ShapeDtypeStruct((bsz * t_len, c), BF16),
        scratch_shapes=[pltpu.VMEM((POOL_CARRY + tt, c), F32)],
        compiler_params=_cparams(2),
        name="pool_mixer",
    )(proj, buf16, w, scale.reshape(1, c))


def _lambda_value(lam_ref, lam_init):
    v = lam_ref[...]
    s1 = jnp.sum(v[0:1] * v[1:2], axis=-1, keepdims=True)
    s2 = jnp.sum(v[2:3] * v[3:4], axis=-1, keepdims=True)
    return jnp.exp(s1) - jnp.exp(s2) + lam_init


def _head_slopes(head_idx_f32):
    return jnp.exp2((-8.0 / DA_HEADS) * (head_idx_f32 + 1.0))


def _subln(o, g, lam_init):
    ms = jnp.mean(o * o, axis=-1, keepdims=True)
    return o * lax.rsqrt(ms + EPS) * g * (1.0 - lam_init)


LOG2E = math.log2(math.e)


def _softmax_tile(s, v_bf16, m_ref, l_ref, acc_ref, idx):
    cols = s.shape[1]
    m_prev = m_ref[idx]
    m_next = jnp.maximum(m_prev, jnp.max(s, axis=-1, keepdims=True))
    alpha = jnp.exp2(m_prev - m_next)
    p = jnp.exp2(s - jnp.tile(m_next, (1, cols // LANES)))
    part = p[:, 0:LANES]
    for i in range(1, cols // LANES):
        part = part + p[:, i * LANES:(i + 1) * LANES]
    l_ref[idx] = alpha * l_ref[idx] + part
    acc_ref[idx] = jnp.tile(alpha, (1, acc_ref.shape[-1] // LANES)) * acc_ref[idx] + jnp.dot(
        p.astype(BF16), v_bf16, preferred_element_type=F32)
    m_ref[idx] = m_next


def _attn_prompt_kernel(q_ref, k_ref, v_ref, lam_ref, g_ref, o_ref, kb_ref, vb_ref, m_ref, l_ref,
                        acc_ref, *, tq, lam_init):
    hd = DA_HEAD_DIM
    h = pl.program_id(1)
    qi = pl.program_id(2)

    @pl.when(qi == 0)
    def _():
        kb_ref[...] = k_ref[...].astype(BF16)
        vb_ref[...] = v_ref[...].astype(BF16)

    slope2 = _head_slopes(jnp.full((1, 1), h, jnp.int32).astype(F32)) * LOG2E
    q = (q_ref[...] * (hd ** -0.5 * LOG2E)).astype(BF16)
    m_ref[...] = jnp.full(m_ref.shape, NEG_INF, F32)
    l_ref[...] = jnp.zeros(l_ref.shape, F32)
    acc_ref[...] = jnp.zeros(acc_ref.shape, F32)
    k_local = lax.broadcasted_iota(jnp.int32, (1, tq), 1)

    def tile(ki, diagonal):
        start = pl.multiple_of(ki * tq, tq)
        k = kb_ref[pl.ds(start, tq), :]
        v = vb_ref[pl.ds(start, tq), :]
        col_bias = slope2 * (k_local + ki * tq).astype(F32)
        for c in range(2):
            s = lax.dot_general(q[:, c * hd:(c + 1) * hd], k[:, c * hd:(c + 1) * hd],
                                (((1,), (1,)), ((), ())), preferred_element_type=F32) + col_bias
            if diagonal:
                keep = (lax.broadcasted_iota(jnp.int32, (tq, tq), 0)
                        >= lax.broadcasted_iota(jnp.int32, (tq, tq), 1))
                s = jnp.where(keep, s, NEG_INF)
            _softmax_tile(s, v, m_ref, l_ref, acc_ref, c)

    def body(ki, carry):
        tile(ki, False)
        return carry

    lax.fori_loop(0, qi, body, 0)
    tile(qi, True)
    lam = _lambda_value(lam_ref, lam_init)
    o = (acc_ref[0] / jnp.sum(l_ref[0], axis=-1, keepdims=True)
         - lam * (acc_ref[1] / jnp.sum(l_ref[1], axis=-1, keepdims=True)))
    o_ref[...] = _subln(o, g_ref[...], lam_init).astype(o_ref.dtype)


def attention_prompt(proj, lam_vecs, subln_g, *, bsz, t_len, q_col, k_col, v_col, tq, lam_init):
    hw = 2 * DA_HEAD_DIM
    nq = t_len // tq
    assert t_len % tq == 0
    return pl.pallas_call(
        functools.partial(_attn_prompt_kernel, tq=tq, lam_init=lam_init),
        grid=(bsz, DA_HEADS, nq),
        in_specs=[
            pl.BlockSpec((tq, hw), lambda b, h, i: (b * nq + i, q_col + h)),
            pl.BlockSpec((t_len, hw), lambda b, h, i: (b, k_col + h)),
            pl.BlockSpec((t_len, hw), lambda b, h, i: (b, v_col + h)),
            pl.BlockSpec((4, DA_HEAD_DIM), lambda b, h, i: (0, 0)),
            pl.BlockSpec((1, hw), lambda b, h, i: (0, 0)),
        ],
        out_specs=pl.BlockSpec((tq, hw), lambda b, h, i: (b * nq + i, h)),
        out_shape=jax.ShapeDtypeStruct((bsz * t_len, DA_HEADS * hw), BF16),
        scratch_shapes=[pltpu.VMEM((t_len, hw), BF16), pltpu.VMEM((t_len, hw), BF16),
                        pltpu.VMEM((2, tq, LANES), F32), pltpu.VMEM((2, tq, LANES), F32),
                        pltpu.VMEM((2, tq, hw), F32)],
        compiler_params=_cparams(3),
        name="attention_prompt",
    )(proj, proj, proj, lam_vecs, subln_g.reshape(1, hw))


def _attn_sample_kernel(pt_ref, q_ref, *refs, n_pages, tpad, pos0, lam_init):
    del pt_ref
    k_refs = refs[:n_pages]
    v_refs = refs[n_pages:2 * n_pages]
    kn_ref, vn_ref, lam_ref, g_ref, o_ref, qall_ref, bias_ref, m_ref, l_ref, acc_ref = refs[2 * n_pages:]
    hd = DA_HEAD_DIM
    nh = DA_HEADS
    page = k_refs[0].shape[2]
    nrow = nh * 2 * tpad
    ncol = page * nh
    sidx = pl.program_id(1)
    sh_row = int(math.log2(2 * tpad))
    sh_col = int(math.log2(nh))
    assert 1 << sh_row == 2 * tpad and 1 << sh_col == nh

    def rel_and_match():
        row = lax.broadcasted_iota(jnp.int32, (nrow, ncol), 0)
        col = lax.broadcasted_iota(jnp.int32, (nrow, ncol), 1)
        rel = (jnp.right_shift(col, sh_col) - jnp.bitwise_and(row, tpad - 1)).astype(F32)
        match = jnp.right_shift(row, sh_row) == jnp.bitwise_and(col, nh - 1)
        return rel, match

    def row_slopes(shape):
        row = lax.broadcasted_iota(jnp.int32, shape, 0)
        return _head_slopes(jnp.right_shift(row, sh_row).astype(F32)) * LOG2E

    @pl.when(sidx == 0)
    def _():
        q = q_ref[...] * (hd ** -0.5 * LOG2E)
        lane = lax.broadcasted_iota(jnp.int32, (tpad, 2 * hd), 1)
        for h in range(nh):
            qh = q[:, h * 2 * hd:(h + 1) * 2 * hd]
            qall_ref[h * 2 * tpad:h * 2 * tpad + tpad, :] = jnp.where(lane < hd, qh, 0.0).astype(BF16)
            qall_ref[h * 2 * tpad + tpad:(h + 1) * 2 * tpad, :] = jnp.where(lane >= hd, qh, 0.0).astype(BF16)
        m_ref[...] = jnp.full(m_ref.shape, NEG_INF, F32)
        l_ref[...] = jnp.zeros(l_ref.shape, F32)
        acc_ref[...] = jnp.zeros(acc_ref.shape, F32)
        rel, match = rel_and_match()
        bias_ref[...] = jnp.where(match, row_slopes((nrow, ncol)) * rel, NEG_INF)

    def scores(k_page):
        kk = k_page.reshape(ncol, 2 * hd).astype(BF16)
        return lax.dot_general(qall_ref[...], kk, (((1,), (1,)), ((), ())), preferred_element_type=F32)

    slope_col = row_slopes((nrow, 1))
    for i in range(n_pages):
        base = ((sidx * n_pages + i) * page - pos0).astype(F32)
        s = scores(k_refs[i][0, 0]) + bias_ref[...] + slope_col * base
        vv = v_refs[i][0, 0].reshape(ncol, 2 * hd).astype(BF16)
        _softmax_tile(s, vv, m_ref, l_ref, acc_ref, 0)

    @pl.when(sidx == pl.num_programs(1) - 1)
    def _():
        rel, _ = rel_and_match()
        s = scores(kn_ref[0]) + jnp.where(rel <= 0.0, bias_ref[...], NEG_INF)
        vv = vn_ref[0].reshape(ncol, 2 * hd).astype(BF16)
        _softmax_tile(s, vv, m_ref, l_ref, acc_ref, 0)
        lam = _lambda_value(lam_ref, lam_init)
        a = acc_ref[0] / jnp.sum(l_ref[0], axis=-1, keepdims=True)
        for h in range(nh):
            r0 = h * 2 * tpad
            o = _subln(a[r0:r0 + tpad] - lam * a[r0 + tpad:r0 + 2 * tpad], g_ref[...], lam_init)
            o_ref[:, h * 2 * hd:(h + 1) * 2 * hd] = o.astype(o_ref.dtype)


def attention_sample(proj, cache_k, cache_v, layer, page_table, k_new, v_new, lam_vecs, subln_g,
                     *, bsz, tpad, q_col, n_pages, lam_init):
    hw = 2 * DA_HEAD_DIM
    qw = DA_HEADS * hw
    page = cache_k.shape[2]
    n_past_pages = page_table.shape[1]
    assert n_past_pages % n_pages == 0
    pos0 = n_past_pages * page
    nrow = DA_HEADS * 2 * tpad
    page_block = (1, 1, page, DA_HEADS, hw)

    def page_spec(i):
        return pl.BlockSpec(page_block, lambda b, s, pt: (layer, pt[b, s * n_pages + i], 0, 0, 0))

    new_spec = pl.BlockSpec((1, page, DA_HEADS, hw), lambda b, s, pt: (b, 0, 0, 0))
    grid_spec = pltpu.PrefetchScalarGridSpec(
        num_scalar_prefetch=1,
        grid=(bsz, n_past_pages // n_pages),
        in_specs=[pl.BlockSpec((tpad, qw), lambda b, s, pt: (b, q_col))]
        + [page_spec(i) for i in range(n_pages)] * 2
        + [new_spec, new_spec,
           pl.BlockSpec((4, DA_HEAD_DIM), lambda b, s, pt: (0, 0)),
           pl.BlockSpec((1, hw), lambda b, s, pt: (0, 0))],
        out_specs=pl.BlockSpec((tpad, qw), lambda b, s, pt: (b, 0)),
        scratch_shapes=[pltpu.VMEM((nrow, hw), BF16), pltpu.VMEM((nrow, page * DA_HEADS), F32),
                        pltpu.VMEM((1, nrow, LANES), F32), pltpu.VMEM((1, nrow, LANES), F32),
                        pltpu.VMEM((1, nrow, hw), F32)],
    )
    return pl.pallas_call(
        functools.partial(_attn_sample_kernel, n_pages=n_pages, tpad=tpad, pos0=pos0, lam_init=lam_init),
        grid_spec=grid_spec,
        out_shape=jax.ShapeDtypeStruct((bsz * tpad, qw), BF16),
        compiler_params=_cparams(2),
        name="attention_sample",
    )(page_table, proj, *([cache_k] * n_pages), *([cache_v] * n_pages), k_new, v_new,
      lam_vecs, subln_g.reshape(1, hw))


def _conformer_kernel(a_ref, gate_ref, buf_ref, dww_ref, dwb_ref, lng_ref, lnb_ref, pww_ref, pwb_ref,
                      z_ref, glu_ref, full_ref, cv_ref, *, tt):
    t = pl.program_id(1)

    @pl.when(t == 0)
    def _():
        full_ref[0:CONV_CARRY, :] = buf_ref[0]

    @pl.when(t != 0)
    def _():
        full_ref[0:CONV_CARRY, :] = full_ref[tt:tt + CONV_CARRY, :]

    glu = a_ref[...] * _sigmoid(gate_ref[...])
    full_ref[CONV_CARRY:CONV_CARRY + tt, :] = glu
    glu_ref[...] = glu
    first = CONV_CARRY - (CONV_K - 1)
    rb = min(tt, 128)
    phases = {}
    for k in range(CONV_K):
        phases.setdefault((first + k) % SUBLANES, []).append(k)

    def lane_chunk(c, carry):
        col = pl.ds(pl.multiple_of(c * LANES, LANES), LANES)
        for r0 in range(0, tt, rb):
            acc = jnp.zeros((rb, LANES), F32) + dwb_ref[:, col]
            for phase, taps in phases.items():
                rows = rb if phase == 0 else rb + SUBLANES
                part = None
                for k in taps:
                    base = r0 + first + k - phase
                    term = dww_ref[k:k + 1, col] * full_ref[base:base + rows, col]
                    part = term if part is None else part + term
                acc = acc + part[phase:phase + rb]
            cv_ref[r0:r0 + rb, col] = acc
        return carry

    lax.fori_loop(0, cv_ref.shape[1] // LANES, lane_chunk, 0)
    y = cv_ref[...]
    mu = jnp.mean(y, axis=-1, keepdims=True)
    yc = y - mu
    var = jnp.mean(yc * yc, axis=-1, keepdims=True)
    y = yc * lax.rsqrt(var + EPS) * lng_ref[...] + lnb_ref[...]
    y = y * _sigmoid(y)
    z = jnp.dot(y.astype(BF16), pww_ref[...], preferred_element_type=F32) + pwb_ref[...]
    z_ref[...] = z.astype(z_ref.dtype)


def conformer_mixer(proj, buf32, dw_w, dw_b, ln_g, ln_b, pw_w, layer, pw_b, *, bsz, t_len, tt):
    c = dw_b.shape[-1]
    nt = t_len // tt
    assert t_len % tt == 0 and (nt == 1 or tt >= CONV_CARRY)
    vec = lambda: pl.BlockSpec((1, c), lambda b, t: (0, 0))
    rows = lambda j: pl.BlockSpec((tt, c), lambda b, t: (b * nt + t, j))
    return pl.pallas_call(
        functools.partial(_conformer_kernel, tt=tt),
        grid=(bsz, nt),
        in_specs=[
            rows(0), rows(1),
            pl.BlockSpec((1, CONV_CARRY, c), lambda b, t: (b, 0, 0)),
            pl.BlockSpec((CONV_K, c), lambda b, t: (0, 0)),
            vec(), vec(), vec(),
            pl.BlockSpec((None, c, c), lambda b, t: (layer, 0, 0)),
            vec(),
        ],
        out_specs=[rows(0), rows(0)],
        out_shape=[jax.ShapeDtypeStruct((bsz * t_len, c), BF16),
                   jax.ShapeDtypeStruct((bsz * t_len, c), F32)],
        scratch_shapes=[pltpu.VMEM((CONV_CARRY + tt, c), F32), pltpu.VMEM((tt, c), F32)],
        compiler_params=_cparams(2),
        name="conformer_mixer",
    )(proj, proj, buf32, dw_w, dw_b.reshape(1, c), ln_g.reshape(1, c), ln_b.reshape(1, c),
      pw_w, pw_b.reshape(1, c))


def _split3(x, axis):
    hi = x.astype(BF16)
    rest = x - hi.astype(F32)
    mid = rest.astype(BF16)
    lo = (rest - mid.astype(F32)).astype(BF16)
    return jnp.concatenate([hi, mid, lo], axis=axis)


def _select_dot_left(sel_bf16, x):
    n = x.shape[1]
    r = jnp.dot(sel_bf16, _split3(x, 1), preferred_element_type=F32)
    return r[:, 0:n] + r[:, n:2 * n] + r[:, 2 * n:3 * n]


def _ssd_kernel(z_ref, x_ref, b_ref, c_ref, dt_ref, buf_ref, cw_ref, cb_ref, dtb_ref, alog_ref,
                dexp_ref, ng_ref, e_ref, et_ref, h0_ref, y_ref, hn_ref, full_ref, st_ref,
                *, t_valid):
    t = pl.program_id(1)
    ln = x_ref.shape[0]
    inner = x_ref.shape[1]
    gn = b_ref.shape[1]
    ns = SSM_STATE
    gw = inner // SSM_GROUPS
    hpg = gw // SSM_HEAD_DIM

    @pl.when(t == 0)
    def _():
        full_ref[0:SSM_CARRY, :] = buf_ref[0]
        st_ref[...] = h0_ref[0]

    @pl.when(t != 0)
    def _():
        full_ref[0:SSM_CARRY, :] = full_ref[ln:ln + SSM_CARRY, :]

    full_ref[SSM_CARRY:SSM_CARRY + ln, 0:inner] = x_ref[...]
    full_ref[SSM_CARRY:SSM_CARRY + ln, inner:inner + gn] = b_ref[...]
    full_ref[SSM_CARRY:SSM_CARRY + ln, inner + gn:inner + 2 * gn] = c_ref[...]
    first = SSM_CARRY - (SSM_CONV_K - 1)
    xc = cb_ref[...] + cw_ref[0:1, :] * full_ref[first:first + ln, :]
    for k in range(1, SSM_CONV_K):
        xc = xc + cw_ref[k:k + 1, :] * full_ref[first + k:first + k + ln, :]
    xc = xc * _sigmoid(xc)
    xs = xc[:, 0:inner]
    bm = xc[:, inner:inner + gn].astype(BF16)
    cm = xc[:, inner + gn:inner + 2 * gn].astype(BF16)

    dt_in = dt_ref[...] + dtb_ref[...]
    dt = jnp.maximum(dt_in, 0.0) + jnp.log1p(jnp.exp(-jnp.abs(dt_in)))
    row = lax.broadcasted_iota(jnp.int32, (ln, ln), 0)
    col = lax.broadcasted_iota(jnp.int32, (ln, ln), 1)
    if t_valid < ln:
        dt = jnp.where(lax.broadcasted_iota(jnp.int32, dt.shape, 0) < t_valid, dt, 0.0)
    ad = dt * (-jnp.exp(alog_ref[...]))
    causal = row >= col
    a_cs = _select_dot_left(jnp.where(causal, 1.0, 0.0).astype(BF16), ad)
    a_cs_t = a_cs.T
    a_last = a_cs[ln - 1:ln, :]
    per_head = jnp.concatenate([dt, jnp.exp(a_last - a_cs), jnp.exp(a_cs)], axis=0)
    per_chan = jnp.dot(_split3(per_head, 1), e_ref[...], preferred_element_type=F32)
    xd = xs * per_chan[0:ln]
    xdw = (xd * per_chan[ln:2 * ln]).astype(BF16)
    xd = xd.astype(BF16)
    exp_acs = per_chan[2 * ln:3 * ln]
    last_t = jnp.broadcast_to(jnp.exp(a_cs_t[:, ln - 1:ln]), a_cs_t.shape)
    st_decay = _select_dot_left(et_ref[...], last_t)

    lane_head = lax.broadcasted_iota(jnp.int32, (ln, gw), 1) // SSM_HEAD_DIM
    for g in range(SSM_GROUPS):
        cols = slice(g * gw, (g + 1) * gw)
        scols = slice(g * ns, (g + 1) * ns)
        scores = lax.dot_general(cm[:, scols], bm[:, scols], (((1,), (1,)), ((), ())),
                                 preferred_element_type=F32)
        lhs, rhs = [], []
        for r in range(hpg):
            h = g * hpg + r
            seg = a_cs[:, h:h + 1] - a_cs_t[h:h + 1, :]
            lhs.append((scores * jnp.where(causal, jnp.exp(seg), 0.0)).astype(BF16))
            rhs.append(jnp.where(lane_head == r, xd[:, cols], jnp.zeros_like(xd[:, cols])))
        y = jnp.dot(jnp.concatenate(lhs, axis=1), jnp.concatenate(rhs, axis=0),
                    preferred_element_type=F32)
        h_prev = st_ref[cols, :]
        y = y + exp_acs[:, cols] * lax.dot_general(
            cm[:, scols], h_prev.astype(BF16), (((1,), (1,)), ((), ())), preferred_element_type=F32)
        new_states = lax.dot_general(xdw[:, cols], bm[:, scols], (((0,), (0,)), ((), ())),
                                     preferred_element_type=F32)
        st_ref[cols, :] = h_prev * st_decay[cols, :] + new_states
        y = y + dexp_ref[:, cols] * xs[:, cols]
        zg = z_ref[:, cols]
        y = y * (zg * _sigmoid(zg))
        y_ref[:, cols] = (_rms_rows(y, ng_ref[:, cols])).astype(y_ref.dtype)

    @pl.when(t == pl.num_programs(1) - 1)
    def _():
        hn_ref[0] = st_ref[...]


def ssd_mixer(proj, dt_raw, buf8, h0, conv_w, conv_b, dt_bias, a_log, d_skip, norm_g,
              *, bsz, t_len, t_valid, z_col, x_col, b_col, c_col):
    inner = norm_g.shape[-1]
    heads = inner // SSM_HEAD_DIM
    gn = SSM_GROUPS * SSM_STATE
    xbc = inner + 2 * gn
    ln = SSM_CHUNK
    nc = t_len // ln
    assert t_len % ln == 0 and heads <= LANES

    def pad_heads(v):
        return jnp.pad(v.astype(F32), (0, LANES - heads)).reshape(1, LANES)

    expand = (jnp.arange(LANES)[:, None] == (jnp.arange(inner) // SSM_HEAD_DIM)[None, :]).astype(BF16)
    d_exp = jnp.repeat(d_skip.astype(F32), SSM_HEAD_DIM).reshape(1, inner)
    const = lambda shape: pl.BlockSpec(shape, lambda b, t: (0,) * len(shape))
    rows = lambda w, j: pl.BlockSpec((ln, w), lambda b, t: (b * nc + t, j))
    return pl.pallas_call(
        functools.partial(_ssd_kernel, t_valid=t_valid),
        grid=(bsz, nc),
        in_specs=[
            rows(inner, z_col), rows(inner, x_col), rows(gn, b_col), rows(gn, c_col),
            rows(LANES, 0),
            pl.BlockSpec((1, SSM_CARRY, xbc), lambda b, t: (b, 0, 0)),
            const((SSM_CONV_K, xbc)), const((1, xbc)),
            const((1, LANES)), const((1, LANES)),
            const((1, inner)), const((1, inner)),
            const((3 * LANES, inner)), const((inner, LANES)),
            pl.BlockSpec((1, inner, SSM_STATE), lambda b, t: (b, 0, 0)),
        ],
        out_specs=[rows(inner, 0), pl.BlockSpec((1, inner, SSM_STATE), lambda b, t: (b, 0, 0))],
        out_shape=[jax.ShapeDtypeStruct((bsz * t_len, inner), BF16),
                   jax.ShapeDtypeStruct((bsz, inner, SSM_STATE), F32)],
        scratch_shapes=[pltpu.VMEM((SSM_CARRY + ln, xbc), F32), pltpu.VMEM((inner, SSM_STATE), F32)],
        compiler_params=_cparams(2),
        name="ssd_mixer",
    )(proj, proj, proj, proj, dt_raw, buf8, conv_w, conv_b.reshape(1, xbc), pad_heads(dt_bias),
      pad_heads(a_log), d_exp, norm_g.reshape(1, inner), jnp.tile(expand, (3, 1)), expand.T, h0)


def _kv_layout_kernel(*refs, n_layers):
    ins = refs[:2 * n_layers]
    ko_ref, vo_ref = refs[2 * n_layers:]
    hw = ko_ref.shape[-1]
    for j in range(n_layers):
        @pl.when(pl.program_id(0) == j)
        def _(j=j):
            for h in range(DA_HEADS):
                ko_ref[0, 0, :, h, :] = ins[2 * j][:, h * hw:(h + 1) * hw]
                vo_ref[0, 0, :, h, :] = ins[2 * j + 1][:, h * hw:(h + 1) * hw]


def kv_layout(projs, *, bsz, t_len, tt, k_col, v_col):
    n_layers = len(projs)
    hw = 2 * DA_HEAD_DIM
    width = DA_HEADS * hw
    nt = t_len // tt
    assert t_len % tt == 0

    def src(j, col):
        return pl.BlockSpec((tt, width), lambda l, b, t: (jnp.where(l == j, b * nt + t, 0), col))

    out_spec = pl.BlockSpec((1, 1, tt, DA_HEADS, hw), lambda l, b, t: (l, b, t, 0, 0))
    out_shape = jax.ShapeDtypeStruct((n_layers, bsz, t_len, DA_HEADS, hw), F32)
    in_specs, args = [], []
    for j, p in enumerate(projs):
        in_specs += [src(j, k_col), src(j, v_col)]
        args += [p, p]
    return pl.pallas_call(
        functools.partial(_kv_layout_kernel, n_layers=n_layers),
        grid=(n_layers, bsz, nt),
        in_specs=in_specs,
        out_specs=[out_spec, out_spec],
        out_shape=[out_shape, out_shape],
        compiler_params=_cparams(3),
        name="kv_layout",
    )(*args)


def _tail_rows(buf, rows, cols, t_valid, n):
    take = min(n, t_valid)
    new = rows[:, t_valid - take:t_valid, cols]
    if take == n:
        return new
    return jnp.concatenate([buf[:, buf.shape[1] - (n - take):], new], axis=1)


def _pad_front(buf, rows):
    return jnp.pad(buf, ((0, 0), (rows - buf.shape[1], 0), (0, 0)))


def _trunk(x, pemb, wts, states, *, bsz, t_len, t_valid, tm, tt, paged):
    depth = pemb.shape[0]
    d_model = x.shape[1]
    pool_w = d_model // 2
    da_w = DA_HEADS * 2 * DA_HEAD_DIM
    hw = 2 * DA_HEAD_DIM
    conv_w = d_model // 2
    inner = d_model // 2
    gn = SSM_GROUPS * SSM_STATE
    xbc_w = inner + 2 * gn
    tn = 1024
    tm2 = min(2 * tm, bsz * t_len)
    pos0 = 0 if paged is None else paged[2].shape[1] * paged[0].shape[2]
    h = x
    outs = dict(k=[], v=[], pool=[], conf=[], mconv=[], ssm=[])
    even_projs = []
    q_col = pool_w // hw

    def pad_seq_rows(a, t_new):
        return jnp.pad(a.reshape(bsz, t_len, -1),
                       ((0, 0), (0, t_new - t_len), (0, 0))).reshape(bsz * t_new, -1)

    for i in range(depth):
        j = i // 2
        if i % 2 == 0:
            proj = norm_matmul(h, wts["g_mix"][i], wts["w_in_even"], j, tm=tm, tn=tn)
            rows = proj.reshape(bsz, t_len, -1)
            pool_buf = states["pool"][j]
            outs["pool"].append(_tail_rows(pool_buf, rows, slice(0, pool_w), t_valid, POOL_BUF))
            z_pool = pool_mixer(proj, _pad_front(pool_buf, POOL_CARRY), wts["pool_w"], j,
                                wts["pool_scale"][j], bsz=bsz, t_len=t_len, tt=tt, pos0=pos0)
            lam_init = 0.8 - 0.6 * math.exp(-0.3 * i)
            lam_vecs = jnp.stack([wts["lambda_q1"][j], wts["lambda_k1"][j],
                                  wts["lambda_q2"][j], wts["lambda_k2"][j]]).astype(F32)
            if paged is None:
                even_projs.append(proj)
                attn = attention_prompt(
                    proj, lam_vecs, wts["subln_g"][j], bsz=bsz, t_len=t_len, q_col=q_col,
                    k_col=q_col + DA_HEADS, v_col=q_col + 2 * DA_HEADS, tq=512, lam_init=lam_init)
            else:
                cache_k, cache_v, page_table = paged
                page = cache_k.shape[2]
                k_seq = rows[:, :t_valid, pool_w + da_w:pool_w + 2 * da_w].reshape(
                    bsz, t_valid, DA_HEADS, hw)
                v_seq = rows[:, :t_valid, pool_w + 2 * da_w:pool_w + 3 * da_w].reshape(
                    bsz, t_valid, DA_HEADS, hw)
                outs["k"].append(k_seq)
                outs["v"].append(v_seq)
                new_page = lambda r: jnp.pad(r, ((0, 0), (0, page - t_valid), (0, 0), (0, 0)))
                attn = attention_sample(
                    proj, cache_k, cache_v, j, page_table, new_page(k_seq), new_page(v_seq),
                    lam_vecs, wts["subln_g"][j], bsz=bsz, tpad=t_len, q_col=pool_w // da_w,
                    n_pages=8, lam_init=lam_init)
            h = matmul2_residual(z_pool, attn, wts["w_out_even"], j, h, tm=tm2, tn=tn // 2)
        else:
            x0 = 2 * conv_w + inner
            proj = norm_matmul(h, wts["g_mix"][i], wts["w_in_odd"], j, tm=tm, tn=tn,
                               n_cols=x0 + xbc_w)
            dt_raw = norm_matmul(h, wts["g_mix"][i], wts["w_dt"], j, tm=tm, tn=LANES)
            rows = proj.reshape(bsz, t_len, -1)
            conf_buf = states["conf"][j]
            zc, glu = conformer_mixer(
                proj, _pad_front(conf_buf, CONV_CARRY), wts["conf_dw_w"][j], wts["conf_dw_b"][j],
                wts["conf_ln_g"][j], wts["conf_ln_b"][j], wts["conf_pw_w"], j, wts["conf_pw_b"][j],
                bsz=bsz, t_len=t_len, tt=tt)
            outs["conf"].append(_tail_rows(conf_buf, glu.reshape(bsz, t_len, conv_w),
                                           slice(None), t_valid, CONV_K - 1))
            mconv_buf = states["mconv"][j]
            outs["mconv"].append(_tail_rows(mconv_buf, rows, slice(x0, x0 + xbc_w), t_valid,
                                            SSM_CONV_K - 1))
            t_ssd = -(-t_len // SSM_CHUNK) * SSM_CHUNK
            proj_ssd = proj
            if t_ssd != t_len:
                proj_ssd = pad_seq_rows(proj, t_ssd)
                dt_raw = pad_seq_rows(dt_raw, t_ssd)
            y, h_new = ssd_mixer(
                proj_ssd, dt_raw, _pad_front(mconv_buf, SSM_CARRY), states["ssm"][j],
                wts["ssm_conv_w"][j], wts["ssm_conv_b"][j], wts["ssm_dt_bias"][j],
                wts["ssm_A_log"][j], wts["ssm_D"][j], wts["ssm_norm_g"][j],
                bsz=bsz, t_len=t_ssd, t_valid=min(t_valid, SSM_CHUNK),
                z_col=2 * conv_w // inner, x_col=x0 // inner, b_col=(x0 + inner) // gn,
                c_col=(x0 + inner + gn) // gn)
            if t_ssd != t_len:
                y = y.reshape(bsz, t_ssd, inner)[:, :t_len].reshape(bsz * t_len, inner)
            outs["ssm"].append(h_new)
            h = matmul2_residual(zc, y, wts["w_out_odd"], j, h, tm=tm2, tn=tn // 2)
        hidden = norm_matmul(h, wts["g_mlp"][i], wts["w_up"], i, tm=tm, tn=tn, act="relu2",
                             out_dtype=BF16)
        h = matmul_k_residual(hidden, wts["w_down"], i, h, tm=tm2, tn=tn, tk=2048)
        h = ple_update(h, wts["g_ple"][i], wts["w_ple_gate"], pemb, wts["w_ple_proj"], i,
                       tm=tm, tn=tn // 2)
    if paged is None:
        outs["k"], outs["v"] = kv_layout(even_projs, bsz=bsz, t_len=t_len, tt=tt,
                                         k_col=pool_w // da_w + 1, v_col=pool_w // da_w + 2)
    else:
        outs["k"], outs["v"] = jnp.stack(outs["k"]), jnp.stack(outs["v"])
    y = rmsnorm(h, wts["g_final"], tm=min(256, bsz * t_len))
    return y, outs


def kernel(x_prompt, x_sample, p_prompt, p_sample, cache_k, cache_v, page_table, state_pool, state_conf_conv, state_ssm_conv, state_ssm, g_mix, g_mlp, g_ple, g_final, w_in_even, pool_w, pool_scale, lambda_q1, lambda_k1, lambda_q2, lambda_k2, subln_g, w_out_even, w_in_odd, conf_dw_w, conf_dw_b, conf_ln_g, conf_ln_b, conf_pw_w, conf_pw_b, ssm_conv_w, ssm_conv_b, ssm_dt_bias, ssm_A_log, ssm_D, ssm_norm_g, w_out_odd, w_up, w_down, w_ple_proj, w_ple_gate):
    bp, tp, d_model = x_prompt.shape
    bs, ts, _ = x_sample.shape
    depth = p_prompt.shape[0]
    n_even, n_odd = (depth + 1) // 2, depth // 2
    inner = ssm_norm_g.shape[-1]
    n_main = w_in_odd.shape[-1] - inner // SSM_HEAD_DIM
    w_dt = jnp.pad(w_in_odd[:, :, n_main:].astype(BF16),
                   ((0, 0), (0, 0), (0, LANES - (w_in_odd.shape[-1] - n_main))))
    wts = dict(
        w_dt=w_dt,
        g_mix=g_mix, g_mlp=g_mlp, g_ple=g_ple, g_final=g_final,
        w_in_even=w_in_even.astype(BF16), pool_w=pool_w.astype(BF16), pool_scale=pool_scale,
        lambda_q1=lambda_q1, lambda_k1=lambda_k1, lambda_q2=lambda_q2, lambda_k2=lambda_k2,
        subln_g=subln_g, w_out_even=w_out_even.astype(BF16),
        w_in_odd=w_in_odd.astype(BF16),
        conf_dw_w=conf_dw_w, conf_dw_b=conf_dw_b, conf_ln_g=conf_ln_g, conf_ln_b=conf_ln_b,
        conf_pw_w=conf_pw_w.astype(BF16), conf_pw_b=conf_pw_b, ssm_conv_w=ssm_conv_w,
        ssm_conv_b=ssm_conv_b, ssm_dt_bias=ssm_dt_bias, ssm_A_log=ssm_A_log, ssm_D=ssm_D,
        ssm_norm_g=ssm_norm_g, w_out_odd=w_out_odd.astype(BF16), w_up=w_up.astype(BF16),
        w_down=w_down.astype(BF16), w_ple_proj=w_ple_proj.astype(BF16),
        w_ple_gate=w_ple_gate.astype(BF16))
    dtype = x_prompt.dtype

    zero_states = dict(
        pool=jnp.zeros((n_even, bp, POOL_BUF, pool_scale.shape[-1]), dtype),
        conf=jnp.zeros((n_odd, bp, CONV_K - 1, conf_dw_b.shape[-1]), dtype),
        mconv=jnp.zeros((n_odd, bp, SSM_CONV_K - 1, ssm_conv_b.shape[-1]), dtype),
        ssm=jnp.zeros((n_odd, bp, inner, SSM_STATE), F32))
    y_p, o_p = _trunk(
        x_prompt.reshape(bp * tp, d_model), p_prompt.reshape(depth, bp * tp, -1).astype(BF16), wts,
        zero_states, bsz=bp, t_len=tp, t_valid=tp, tm=512, tt=256, paged=None)

    t_pad = SAMPLE_T_PAD
    pad_t = lambda a, axis: jnp.pad(a, [(0, t_pad - ts) if d == axis else (0, 0) for d in range(a.ndim)])
    sample_states = dict(pool=state_pool, conf=state_conf_conv, mconv=state_ssm_conv,
                         ssm=state_ssm.astype(F32).reshape(n_odd, bs, inner, SSM_STATE))
    y_s, o_s = _trunk(
        pad_t(x_sample, 1).reshape(bs * t_pad, d_model),
        pad_t(p_sample, 2).reshape(depth, bs * t_pad, -1).astype(BF16), wts, sample_states,
        bsz=bs, t_len=t_pad, t_valid=ts, tm=bs * t_pad, tt=t_pad,
        paged=(cache_k, cache_v, page_table))

    def ssm_out(states, bsz):
        return jnp.stack(states).reshape(n_odd, bsz, inner // SSM_HEAD_DIM, SSM_HEAD_DIM,
                                         SSM_STATE).astype(state_ssm.dtype)

    return (y_p.reshape(bp, tp, d_model), y_s.reshape(bs, t_pad, d_model)[:, :ts],
            o_p["k"], o_p["v"], o_s["k"], o_s["v"],
            jnp.stack(o_p["pool"]), jnp.stack(o_s["pool"]),
            jnp.stack(o_p["conf"]), jnp.stack(o_s["conf"]),
            jnp.stack(o_p["mconv"]), jnp.stack(o_s["mconv"]),
            ssm_out(o_p["ssm"], bp), ssm_out(o_s["ssm"], bs))
```

```python
import functools
import math

import jax
import jax.numpy as jnp
from jax import lax
from jax.experimental import pallas as pl
from jax.experimental.pallas import tpu as pltpu

F32 = jnp.float32
BF16 = jnp.bfloat16
EPS = 1e-6
NEG_INF = -1e30

V7X_VMEM_BYTES = 64 * 1024 * 1024
VMEM_LIMIT = V7X_VMEM_BYTES - 4 * 1024 * 1024

LANES = 128
SUBLANES = 8
POOL_WINDOWS = (2, 4, 8, 16)
POOL_BUF = max(POOL_WINDOWS) - 1
POOL_CARRY = 16
CONV_K = 31
CONV_CARRY = 32
SSM_CONV_K = 4
SSM_CARRY = 8
SSM_CHUNK = 128
DA_HEADS = 8
DA_HEAD_DIM = 128
SSM_HEAD_DIM = 64
SSM_GROUPS = 8
SSM_STATE = 128
SAMPLE_T_PAD = 16


def _cparams(n_axes):
    return pltpu.CompilerParams(
        dimension_semantics=("arbitrary",) * n_axes, vmem_limit_bytes=VMEM_LIMIT)


def _rms_rows(x, g):
    ms = jnp.mean(x * x, axis=-1, keepdims=True)
    return x * lax.rsqrt(ms + EPS) * g


def _sigmoid(x):
    return 0.5 * jnp.tanh(0.5 * x) + 0.5


def _normalise_row_tile(x_hbm, g_ref, xbuf_ref, sem_ref, xn_ref):
    i = pl.program_id(0)
    tm = xn_ref.shape[0]

    def row_copy(tile, slot):
        return pltpu.make_async_copy(x_hbm.at[pl.ds(tile * tm, tm), :], xbuf_ref.at[slot],
                                     sem_ref.at[slot])

    @pl.when(pl.program_id(1) == 0)
    def _():
        slot = lax.rem(i, 2)

        @pl.when(i == 0)
        def _():
            row_copy(0, 0).start()

        row_copy(i, slot).wait()

        @pl.when(i + 1 < pl.num_programs(0))
        def _():
            row_copy(i + 1, 1 - slot).start()

        xn_ref[...] = _rms_rows(xbuf_ref[slot], g_ref[...]).astype(BF16)


def _row_tile_scratch(tm, k):
    return [pltpu.VMEM((2, tm, k), F32), pltpu.SemaphoreType.DMA((2,)), pltpu.VMEM((tm, k), BF16)]


def _norm_mm_kernel(x_hbm, g_ref, w_ref, o_ref, xbuf_ref, sem_ref, xn_ref, *, act):
    _normalise_row_tile(x_hbm, g_ref, xbuf_ref, sem_ref, xn_ref)
    acc = jnp.dot(xn_ref[...], w_ref[...], preferred_element_type=F32)
    if act == "relu2":
        acc = jnp.square(jnp.maximum(acc, 0.0))
    o_ref[...] = acc.astype(o_ref.dtype)


def norm_matmul(x, g, w, layer, *, tm, tn, n_cols=None, act=None, out_dtype=F32):
    m, k = x.shape
    n = w.shape[2] if n_cols is None else n_cols
    assert m % tm == 0 and n % tn == 0
    return pl.pallas_call(
        functools.partial(_norm_mm_kernel, act=act),
        grid=(m // tm, n // tn),
        in_specs=[
            pl.BlockSpec(memory_space=pl.ANY),
            pl.BlockSpec((1, k), lambda i, j: (0, 0)),
            pl.BlockSpec((None, k, tn), lambda i, j: (layer, 0, j)),
        ],
        out_specs=pl.BlockSpec((tm, tn), lambda i, j: (i, j)),
        out_shape=jax.ShapeDtypeStruct((m, n), out_dtype),
        scratch_shapes=_row_tile_scratch(tm, k),
        compiler_params=_cparams(2),
        name="norm_matmul",
    )(x, g.reshape(1, k), w)


def _mm2_res_kernel(a1_ref, a2_ref, w_ref, r_ref, o_ref):
    k1 = a1_ref.shape[1]
    acc = jnp.dot(a1_ref[...], w_ref[:k1, :], preferred_element_type=F32)
    acc += jnp.dot(a2_ref[...], w_ref[k1:, :], preferred_element_type=F32)
    o_ref[...] = r_ref[...] + acc


def matmul2_residual(a1, a2, w, layer, res, *, tm, tn):
    m, k1 = a1.shape
    k2 = a2.shape[1]
    n = w.shape[2]
    assert m % tm == 0 and n % tn == 0 and w.shape[1] == k1 + k2
    return pl.pallas_call(
        _mm2_res_kernel,
        grid=(m // tm, n // tn),
        in_specs=[
            pl.BlockSpec((tm, k1), lambda i, j: (i, 0)),
            pl.BlockSpec((tm, k2), lambda i, j: (i, 0)),
            pl.BlockSpec((None, k1 + k2, tn), lambda i, j: (layer, 0, j)),
            pl.BlockSpec((tm, tn), lambda i, j: (i, j)),
        ],
        out_specs=pl.BlockSpec((tm, tn), lambda i, j: (i, j)),
        out_shape=jax.ShapeDtypeStruct((m, n), F32),
        compiler_params=_cparams(2),
        name="matmul2_residual",
    )(a1, a2, w, res)


def _mmk_res_kernel(a_ref, w_ref, r_ref, o_ref):
    @pl.when(pl.program_id(2) == 0)
    def _():
        o_ref[...] = r_ref[...]

    o_ref[...] += jnp.dot(a_ref[...], w_ref[...], preferred_element_type=F32)


def matmul_k_residual(a, w, layer, res, *, tm, tn, tk):
    m, k = a.shape
    n = w.shape[2]
    assert m % tm == 0 and n % tn == 0 and k % tk == 0
    return pl.pallas_call(
        _mmk_res_kernel,
        grid=(m // tm, n // tn, k // tk),
        in_specs=[
            pl.BlockSpec((tm, tk), lambda i, j, l: (i, l)),
            pl.BlockSpec((None, tk, tn), lambda i, j, l: (layer, l, j)),
            pl.BlockSpec((tm, tn), lambda i, j, l: (i, j)),
        ],
        out_specs=pl.BlockSpec((tm, tn), lambda i, j, l: (i, j)),
        out_shape=jax.ShapeDtypeStruct((m, n), F32),
        compiler_params=_cparams(3),
        name="matmul_k_residual",
    )(a, w, res)


def _ple_kernel(x_hbm, g_ref, wg_ref, pe_ref, wp_ref, r_ref, o_ref, xbuf_ref, sem_ref, xn_ref):
    _normalise_row_tile(x_hbm, g_ref, xbuf_ref, sem_ref, xn_ref)
    gate = jax.nn.sigmoid(jnp.dot(xn_ref[...], wg_ref[...], preferred_element_type=F32))
    proj = jnp.dot(pe_ref[...], wp_ref[...], preferred_element_type=F32)
    o_ref[...] = r_ref[...] + gate * proj


def ple_update(h, g, wg, pe, wp, layer, *, tm, tn):
    m, k = h.shape
    n = wg.shape[2]
    kp = pe.shape[2]
    assert m % tm == 0 and n % tn == 0
    return pl.pallas_call(
        _ple_kernel,
        grid=(m // tm, n // tn),
        in_specs=[
            pl.BlockSpec(memory_space=pl.ANY),
            pl.BlockSpec((1, k), lambda i, j: (0, 0)),
            pl.BlockSpec((None, k, tn), lambda i, j: (layer, 0, j)),
            pl.BlockSpec((None, tm, kp), lambda i, j: (layer, i, 0)),
            pl.BlockSpec((None, kp, tn), lambda i, j: (layer, 0, j)),
            pl.BlockSpec((tm, tn), lambda i, j: (i, j)),
        ],
        out_specs=pl.BlockSpec((tm, tn), lambda i, j: (i, j)),
        out_shape=jax.ShapeDtypeStruct((m, n), F32),
        scratch_shapes=_row_tile_scratch(tm, k),
        compiler_params=_cparams(2),
        name="ple_update",
    )(h, g.reshape(1, k), wg, pe, wp, h)


def _rmsnorm_kernel(x_ref, g_ref, o_ref):
    o_ref[...] = _rms_rows(x_ref[...], g_ref[...])


def rmsnorm(x, g, *, tm):
    m, k = x.shape
    assert m % tm == 0
    return pl.pallas_call(
        _rmsnorm_kernel,
        grid=(m // tm,),
        in_specs=[pl.BlockSpec((tm, k), lambda i: (i, 0)), pl.BlockSpec((1, k), lambda i: (0, 0))],
        out_specs=pl.BlockSpec((tm, k), lambda i: (i, 0)),
        out_shape=jax.ShapeDtypeStruct((m, k), F32),
        compiler_params=_cparams(1),
        name="final_rmsnorm",
    )(x, g.reshape(1, k))


def _pool_kernel(u_ref, buf_ref, w_ref, scale_ref, z_ref, full_ref, *, tt, pos0):
    t = pl.program_id(1)

    @pl.when(t == 0)
    def _():
        full_ref[0:POOL_CARRY, :] = buf_ref[0]

    @pl.when(t != 0)
    def _():
        full_ref[0:POOL_CARRY, :] = full_ref[tt:tt + POOL_CARRY, :]

    full_ref[POOL_CARRY:POOL_CARRY + tt, :] = u_ref[...]
    gc = w_ref.shape[1]
    pos = pos0 + t * tt + lax.broadcasted_iota(jnp.int32, (tt, 1), 0)
    for g, win in enumerate(POOL_WINDOWS):
        cols = slice(g * gc, (g + 1) * gc)
        cur = full_ref[POOL_CARRY:POOL_CARRY + tt, cols]
        tot = cur
        for i in range(1, win):
            tot = tot + full_ref[POOL_CARRY - i:POOL_CARRY - i + tt, cols]
        cnt = jnp.minimum(pos + 1, win).astype(F32)
        d = tot / cnt - cur
        z = jnp.dot(d.astype(BF16), w_ref[g], preferred_element_type=F32)
        z_ref[:, cols] = (z * scale_ref[:, cols]).astype(z_ref.dtype)


def pool_mixer(proj, buf16, w, layer, scale, *, bsz, t_len, tt, pos0):
    c = scale.shape[-1]
    nt = t_len // tt
    assert t_len % tt == 0 and (nt == 1 or tt >= POOL_CARRY)
    return pl.pallas_call(
        functools.partial(_pool_kernel, tt=tt, pos0=pos0),
        grid=(bsz, nt),
        in_specs=[
            pl.BlockSpec((tt, c), lambda b, t: (b * nt + t, 0)),
            pl.BlockSpec((1, POOL_CARRY, c), lambda b, t: (b, 0, 0)),
            pl.BlockSpec((None,) + w.shape[1:], lambda b, t: (layer, 0, 0, 0)),
            pl.BlockSpec((1, c), lambda b, t: (0, 0)),
        ],
        out_specs=pl.BlockSpec((tt, c), lambda b, t: (b * nt + t, 0)),
        out_shape=jax.ShapeDtypeStruct((bsz * t_len, c), BF16),
        scratch_shapes=[pltpu.VMEM((POOL_CARRY + tt, c), F32)],
        compiler_params=_cparams(2),
        name="pool_mixer",
    )(proj, buf16, w, scale.reshape(1, c))


def _lambda_value(lam_ref, lam_init):
    v = lam_ref[...]
    s1 = jnp.sum(v[0:1] * v[1:2], axis=-1, keepdims=True)
    s2 = jnp.sum(v[2:3] * v[3:4], axis=-1, keepdims=True)
    return jnp.exp(s1) - jnp.exp(s2) + lam_init


def _head_slopes(head_idx_f32):
    return jnp.exp2((-8.0 / DA_HEADS) * (head_idx_f32 + 1.0))


def _subln(o, g, lam_init):
    ms = jnp.mean(o * o, axis=-1, keepdims=True)
    return o * lax.rsqrt(ms + EPS) * g * (1.0 - lam_init)


LOG2E = math.log2(math.e)


def _softmax_tile(s, v_bf16, m_ref, l_ref, acc_ref, idx):
    cols = s.shape[1]
    m_prev = m_ref[idx]
    m_next = jnp.maximum(m_prev, jnp.max(s, axis=-1, keepdims=True))
    alpha = jnp.exp2(m_prev - m_next)
    p = jnp.exp2(s - jnp.tile(m_next, (1, cols // LANES)))
    part = p[:, 0:LANES]
    for i in range(1, cols // LANES):
        part = part + p[:, i * LANES:(i + 1) * LANES]
    l_ref[idx] = alpha * l_ref[idx] + part
    acc_ref[idx] = jnp.tile(alpha, (1, acc_ref.shape[-1] // LANES)) * acc_ref[idx] + jnp.dot(
        p.astype(BF16), v_bf16, preferred_element_type=F32)
    m_ref[idx] = m_next


def _attn_prompt_kernel(q_ref, k_ref, v_ref, lam_ref, g_ref, o_ref, kb_ref, vb_ref, m_ref, l_ref,
                        acc_ref, *, tq, lam_init):
    hd = DA_HEAD_DIM
    h = pl.program_id(1)
    qi = pl.program_id(2)

    @pl.when(qi == 0)
    def _():
        kb_ref[...] = k_ref[...].astype(BF16)
        vb_ref[...] = v_ref[...].astype(BF16)

    slope2 = _head_slopes(jnp.full((1, 1), h, jnp.int32).astype(F32)) * LOG2E
    q = (q_ref[...] * (hd ** -0.5 * LOG2E)).astype(BF16)
    m_ref[...] = jnp.full(m_ref.shape, NEG_INF, F32)
    l_ref[...] = jnp.zeros(l_ref.shape, F32)
    acc_ref[...] = jnp.zeros(acc_ref.shape, F32)
    k_local = lax.broadcasted_iota(jnp.int32, (1, tq), 1)

    def tile(ki, diagonal):
        start = pl.multiple_of(ki * tq, tq)
        k = kb_ref[pl.ds(start, tq), :]
        v = vb_ref[pl.ds(start, tq), :]
        col_bias = slope2 * (k_local + ki * tq).astype(F32)
        for c in range(2):
            s = lax.dot_general(q[:, c * hd:(c + 1) * hd], k[:, c * hd:(c + 1) * hd],
                                (((1,), (1,)), ((), ())), preferred_element_type=F32) + col_bias
            if diagonal:
                keep = (lax.broadcasted_iota(jnp.int32, (tq, tq), 0)
                        >= lax.broadcasted_iota(jnp.int32, (tq, tq), 1))
                s = jnp.where(keep, s, NEG_INF)
            _softmax_tile(s, v, m_ref, l_ref, acc_ref, c)

    def body(ki, carry):
        tile(ki, False)
        return carry

    lax.fori_loop(0, qi, body, 0)
    tile(qi, True)
    lam = _lambda_value(lam_ref, lam_init)
    o = (acc_ref[0] / jnp.sum(l_ref[0], axis=-1, keepdims=True)
         - lam * (acc_ref[1] / jnp.sum(l_ref[1], axis=-1, keepdims=True)))
    o_ref[...] = _subln(o, g_ref[...], lam_init).astype(o_ref.dtype)


def attention_prompt(proj, lam_vecs, subln_g, *, bsz, t_len, q_col, k_col, v_col, tq, lam_init):
    hw = 2 * DA_HEAD_DIM
    nq = t_len // tq
    assert t_len % tq == 0
    return pl.pallas_call(
        functools.partial(_attn_prompt_kernel, tq=tq, lam_init=lam_init),
        grid=(bsz, DA_HEADS, nq),
        in_specs=[
            pl.BlockSpec((tq, hw), lambda b, h, i: (b * nq + i, q_col + h)),
            pl.BlockSpec((t_len, hw), lambda b, h, i: (b, k_col + h)),
            pl.BlockSpec((t_len, hw), lambda b, h, i: (b, v_col + h)),
            pl.BlockSpec((4, DA_HEAD_DIM), lambda b, h, i: (0, 0)),
            pl.BlockSpec((1, hw), lambda b, h, i: (0, 0)),
        ],
        out_specs=pl.BlockSpec((tq, hw), lambda b, h, i: (b * nq + i, h)),
        out_shape=jax.ShapeDtypeStruct((bsz * t_len, DA_HEADS * hw), BF16),
        scratch_shapes=[pltpu.VMEM((t_len, hw), BF16), pltpu.VMEM((t_len, hw), BF16),
                        pltpu.VMEM((2, tq, LANES), F32), pltpu.VMEM((2, tq, LANES), F32),
                        pltpu.VMEM((2, tq, hw), F32)],
        compiler_params=_cparams(3),
        name="attention_prompt",
    )(proj, proj, proj, lam_vecs, subln_g.reshape(1, hw))


def _attn_sample_kernel(pt_ref, q_ref, *refs, n_pages, tpad, pos0, lam_init):
    del pt_ref
    k_refs = refs[:n_pages]
    v_refs = refs[n_pages:2 * n_pages]
    kn_ref, vn_ref, lam_ref, g_ref, o_ref, qall_ref, bias_ref, m_ref, l_ref, acc_ref = refs[2 * n_pages:]
    hd = DA_HEAD_DIM
    nh = DA_HEADS
    page = k_refs[0].shape[2]
    nrow = nh * 2 * tpad
    ncol = page * nh
    sidx = pl.program_id(1)
    sh_row = int(math.log2(2 * tpad))
    sh_col = int(math.log2(nh))
    assert 1 << sh_row == 2 * tpad and 1 << sh_col == nh

    def rel_and_match():
        row = lax.broadcasted_iota(jnp.int32, (nrow, ncol), 0)
        col = lax.broadcasted_iota(jnp.int32, (nrow, ncol), 1)
        rel = (jnp.right_shift(col, sh_col) - jnp.bitwise_and(row, tpad - 1)).astype(F32)
        match = jnp.right_shift(row, sh_row) == jnp.bitwise_and(col, nh - 1)
        return rel, match

    def row_slopes(shape):
        row = lax.broadcasted_iota(jnp.int32, shape, 0)
        return _head_slopes(jnp.right_shift(row, sh_row).astype(F32)) * LOG2E

    @pl.when(sidx == 0)
    def _():
        q = q_ref[...] * (hd ** -0.5 * LOG2E)
        lane = lax.broadcasted_iota(jnp.int32, (tpad, 2 * hd), 1)
        for h in range(nh):
            qh = q[:, h * 2 * hd:(h + 1) * 2 * hd]
            qall_ref[h * 2 * tpad:h * 2 * tpad + tpad, :] = jnp.where(lane < hd, qh, 0.0).astype(BF16)
            qall_ref[h * 2 * tpad + tpad:(h + 1) * 2 * tpad, :] = jnp.where(lane >= hd, qh, 0.0).astype(BF16)
        m_ref[...] = jnp.full(m_ref.shape, NEG_INF, F32)
        l_ref[...] = jnp.zeros(l_ref.shape, F32)
        acc_ref[...] = jnp.zeros(acc_ref.shape, F32)
        rel, match = rel_and_match()
        bias_ref[...] = jnp.where(match, row_slopes((nrow, ncol)) * rel, NEG_INF)

    def scores(k_page):
        kk = k_page.reshape(ncol, 2 * hd).astype(BF16)
        return lax.dot_general(qall_ref[...], kk, (((1,), (1,)), ((), ())), preferred_element_type=F32)

    slope_col = row_slopes((nrow, 1))
    for i in range(n_pages):
        base = ((sidx * n_pages + i) * page - pos0).astype(F32)
        s = scores(k_refs[i][0, 0]) + bias_ref[...] + slope_col * base
        vv = v_refs[i][0, 0].reshape(ncol, 2 * hd).astype(BF16)
        _softmax_tile(s, vv, m_ref, l_ref, acc_ref, 0)

    @pl.when(sidx == pl.num_programs(1) - 1)
    def _():
        rel, _ = rel_and_match()
        s = scores(kn_ref[0]) + jnp.where(rel <= 0.0, bias_ref[...], NEG_INF)
        vv = vn_ref[0].reshape(ncol, 2 * hd).astype(BF16)
        _softmax_tile(s, vv, m_ref, l_ref, acc_ref, 0)
        lam = _lambda_value(lam_ref, lam_init)
        a = acc_ref[0] / jnp.sum(l_ref[0], axis=-1, keepdims=True)
        for h in range(nh):
            r0 = h * 2 * tpad
            o = _subln(a[r0:r0 + tpad] - lam * a[r0 + tpad:r0 + 2 * tpad], g_ref[...], lam_init)
            o_ref[:, h * 2 * hd:(h + 1) * 2 * hd] = o.astype(o_ref.dtype)


def attention_sample(proj, cache_k, cache_v, layer, page_table, k_new, v_new, lam_vecs, subln_g,
                     *, bsz, tpad, q_col, n_pages, lam_init):
    hw = 2 * DA_HEAD_DIM
    qw = DA_HEADS * hw
    page = cache_k.shape[2]
    n_past_pages = page_table.shape[1]
    assert n_past_pages % n_pages == 0
    pos0 = n_past_pages * page
    nrow = DA_HEADS * 2 * tpad
    page_block = (1, 1, page, DA_HEADS, hw)

    def page_spec(i):
        return pl.BlockSpec(page_block, lambda b, s, pt: (layer, pt[b, s * n_pages + i], 0, 0, 0))

    new_spec = pl.BlockSpec((1, page, DA_HEADS, hw), lambda b, s, pt: (b, 0, 0, 0))
    grid_spec = pltpu.PrefetchScalarGridSpec(
        num_scalar_prefetch=1,
        grid=(bsz, n_past_pages // n_pages),
        in_specs=[pl.BlockSpec((tpad, qw), lambda b, s, pt: (b, q_col))]
        + [page_spec(i) for i in range(n_pages)] * 2
        + [new_spec, new_spec,
           pl.BlockSpec((4, DA_HEAD_DIM), lambda b, s, pt: (0, 0)),
           pl.BlockSpec((1, hw), lambda b, s, pt: (0, 0))],
        out_specs=pl.BlockSpec((tpad, qw), lambda b, s, pt: (b, 0)),
        scratch_shapes=[pltpu.VMEM((nrow, hw), BF16), pltpu.VMEM((nrow, page * DA_HEADS), F32),
                        pltpu.VMEM((1, nrow, LANES), F32), pltpu.VMEM((1, nrow, LANES), F32),
                        pltpu.VMEM((1, nrow, hw), F32)],
    )
    return pl.pallas_call(
        functools.partial(_attn_sample_kernel, n_pages=n_pages, tpad=tpad, pos0=pos0, lam_init=lam_init),
        grid_spec=grid_spec,
        out_shape=jax.ShapeDtypeStruct((bsz * tpad, qw), BF16),
        compiler_params=_cparams(2),
        name="attention_sample",
    )(page_table, proj, *([cache_k] * n_pages), *([cache_v] * n_pages), k_new, v_new,
      lam_vecs, subln_g.reshape(1, hw))


def _conformer_kernel(a_ref, gate_ref, buf_ref, dww_ref, dwb_ref, lng_ref, lnb_ref, pww_ref, pwb_ref,
                      z_ref, glu_ref, full_ref, cv_ref, *, tt):
    t = pl.program_id(1)

    @pl.when(t == 0)
    def _():
        full_ref[0:CONV_CARRY, :] = buf_ref[0]

    @pl.when(t != 0)
    def _():
        full_ref[0:CONV_CARRY, :] = full_ref[tt:tt + CONV_CARRY, :]

    glu = a_ref[...] * _sigmoid(gate_ref[...])
    full_ref[CONV_CARRY:CONV_CARRY + tt, :] = glu
    glu_ref[...] = glu
    first = CONV_CARRY - (CONV_K - 1)
    rb = min(tt, 128)
    phases = {}
    for k in range(CONV_K):
        phases.setdefault((first + k) % SUBLANES, []).append(k)

    def lane_chunk(c, carry):
        col = pl.ds(pl.multiple_of(c * LANES, LANES), LANES)
        for r0 in range(0, tt, rb):
            acc = jnp.zeros((rb, LANES), F32) + dwb_ref[:, col]
            for phase, taps in phases.items():
                rows = rb if phase == 0 else rb + SUBLANES
                part = None
                for k in taps:
                    base = r0 + first + k - phase
                    term = dww_ref[k:k + 1, col] * full_ref[base:base + rows, col]
                    part = term if part is None else part + term
                acc = acc + part[phase:phase + rb]
            cv_ref[r0:r0 + rb, col] = acc
        return carry

    lax.fori_loop(0, cv_ref.shape[1] // LANES, lane_chunk, 0)
    y = cv_ref[...]
    mu = jnp.mean(y, axis=-1, keepdims=True)
    yc = y - mu
    var = jnp.mean(yc * yc, axis=-1, keepdims=True)
    y = yc * lax.rsqrt(var + EPS) * lng_ref[...] + lnb_ref[...]
    y = y * _sigmoid(y)
    z = jnp.dot(y.astype(BF16), pww_ref[...], preferred_element_type=F32) + pwb_ref[...]
    z_ref[...] = z.astype(z_ref.dtype)


def conformer_mixer(proj, buf32, dw_w, dw_b, ln_g, ln_b, pw_w, layer, pw_b, *, bsz, t_len, tt):
    c = dw_b.shape[-1]
    nt = t_len // tt
    assert t_len % tt == 0 and (nt == 1 or tt >= CONV_CARRY)
    vec = lambda: pl.BlockSpec((1, c), lambda b, t: (0, 0))
    rows = lambda j: pl.BlockSpec((tt, c), lambda b, t: (b * nt + t, j))
    return pl.pallas_call(
        functools.partial(_conformer_kernel, tt=tt),
        grid=(bsz, nt),
        in_specs=[
            rows(0), rows(1),
            pl.BlockSpec((1, CONV_CARRY, c), lambda b, t: (b, 0, 0)),
            pl.BlockSpec((CONV_K, c), lambda b, t: (0, 0)),
            vec(), vec(), vec(),
            pl.BlockSpec((None, c, c), lambda b, t: (layer, 0, 0)),
            vec(),
        ],
        out_specs=[rows(0), rows(0)],
        out_shape=[jax.ShapeDtypeStruct((bsz * t_len, c), BF16),
                   jax.ShapeDtypeStruct((bsz * t_len, c), F32)],
        scratch_shapes=[pltpu.VMEM((CONV_CARRY + tt, c), F32), pltpu.VMEM((tt, c), F32)],
        compiler_params=_cparams(2),
        name="conformer_mixer",
    )(proj, proj, buf32, dw_w, dw_b.reshape(1, c), ln_g.reshape(1, c), ln_b.reshape(1, c),
      pw_w, pw_b.reshape(1, c))


def _split3(x, axis):
    hi = x.astype(BF16)
    rest = x - hi.astype(F32)
    mid = rest.astype(BF16)
    lo = (rest - mid.astype(F32)).astype(BF16)
    return jnp.concatenate([hi, mid, lo], axis=axis)


def _select_dot_left(sel_bf16, x):
    n = x.shape[1]
    r = jnp.dot(sel_bf16, _split3(x, 1), preferred_element_type=F32)
    return r[:, 0:n] + r[:, n:2 * n] + r[:, 2 * n:3 * n]


def _ssd_kernel(z_ref, x_ref, b_ref, c_ref, dt_ref, buf_ref, cw_ref, cb_ref, dtb_ref, alog_ref,
                dexp_ref, ng_ref, e_ref, et_ref, h0_ref, y_ref, hn_ref, full_ref, st_ref,
                *, t_valid):
    t = pl.program_id(1)
    ln = x_ref.shape[0]
    inner = x_ref.shape[1]
    gn = b_ref.shape[1]
    ns = SSM_STATE
    gw = inner // SSM_GROUPS
    hpg = gw // SSM_HEAD_DIM

    @pl.when(t == 0)
    def _():
        full_ref[0:SSM_CARRY, :] = buf_ref[0]
        st_ref[...] = h0_ref[0]

    @pl.when(t != 0)
    def _():
        full_ref[0:SSM_CARRY, :] = full_ref[ln:ln + SSM_CARRY, :]

    full_ref[SSM_CARRY:SSM_CARRY + ln, 0:inner] = x_ref[...]
    full_ref[SSM_CARRY:SSM_CARRY + ln, inner:inner + gn] = b_ref[...]
    full_ref[SSM_CARRY:SSM_CARRY + ln, inner + gn:inner + 2 * gn] = c_ref[...]
    first = SSM_CARRY - (SSM_CONV_K - 1)
    xc = cb_ref[...] + cw_ref[0:1, :] * full_ref[first:first + ln, :]
    for k in range(1, SSM_CONV_K):
        xc = xc + cw_ref[k:k + 1, :] * full_ref[first + k:first + k + ln, :]
    xc = xc * _sigmoid(xc)
    xs = xc[:, 0:inner]
    bm = xc[:, inner:inner + gn].astype(BF16)
    cm = xc[:, inner + gn:inner + 2 * gn].astype(BF16)

    dt_in = dt_ref[...] + dtb_ref[...]
    dt = jnp.maximum(dt_in, 0.0) + jnp.log1p(jnp.exp(-jnp.abs(dt_in)))
    row = lax.broadcasted_iota(jnp.int32, (ln, ln), 0)
    col = lax.broadcasted_iota(jnp.int32, (ln, ln), 1)
    if t_valid < ln:
        dt = jnp.where(lax.broadcasted_iota(jnp.int32, dt.shape, 0) < t_valid, dt, 0.0)
    ad = dt * (-jnp.exp(alog_ref[...]))
    causal = row >= col
    a_cs = _select_dot_left(jnp.where(causal, 1.0, 0.0).astype(BF16), ad)
    a_cs_t = a_cs.T
    a_last = a_cs[ln - 1:ln, :]
    per_head = jnp.concatenate([dt, jnp.exp(a_last - a_cs), jnp.exp(a_cs)], axis=0)
    per_chan = jnp.dot(_split3(per_head, 1), e_ref[...], preferred_element_type=F32)
    xd = xs * per_chan[0:ln]
    xdw = (xd * per_chan[ln:2 * ln]).astype(BF16)
    xd = xd.astype(BF16)
    exp_acs = per_chan[2 * ln:3 * ln]
    last_t = jnp.broadcast_to(jnp.exp(a_cs_t[:, ln - 1:ln]), a_cs_t.shape)
    st_decay = _select_dot_left(et_ref[...], last_t)

    lane_head = lax.broadcasted_iota(jnp.int32, (ln, gw), 1) // SSM_HEAD_DIM
    for g in range(SSM_GROUPS):
        cols = slice(g * gw, (g + 1) * gw)
        scols = slice(g * ns, (g + 1) * ns)
        scores = lax.dot_general(cm[:, scols], bm[:, scols], (((1,), (1,)), ((), ())),
                                 preferred_element_type=F32)
        lhs, rhs = [], []
        for r in range(hpg):
            h = g * hpg + r
            seg = a_cs[:, h:h + 1] - a_cs_t[h:h + 1, :]
            lhs.append((scores * jnp.where(causal, jnp.exp(seg), 0.0)).astype(BF16))
            rhs.append(jnp.where(lane_head == r, xd[:, cols], jnp.zeros_like(xd[:, cols])))
        y = jnp.dot(jnp.concatenate(lhs, axis=1), jnp.concatenate(rhs, axis=0),
                    preferred_element_type=F32)
        h_prev = st_ref[cols, :]
        y = y + exp_acs[:, cols] * lax.dot_general(
            cm[:, scols], h_prev.astype(BF16), (((1,), (1,)), ((), ())), preferred_element_type=F32)
        new_states = lax.dot_general(xdw[:, cols], bm[:, scols], (((0,), (0,)), ((), ())),
                                     preferred_element_type=F32)
        st_ref[cols, :] = h_prev * st_decay[cols, :] + new_states
        y = y + dexp_ref[:, cols] * xs[:, cols]
        zg = z_ref[:, cols]
        y = y * (zg * _sigmoid(zg))
        y_ref[:, cols] = (_rms_rows(y, ng_ref[:, cols])).astype(y_ref.dtype)

    @pl.when(t == pl.num_programs(1) - 1)
    def _():
        hn_ref[0] = st_ref[...]


def ssd_mixer(proj, dt_raw, buf8, h0, conv_w, conv_b, dt_bias, a_log, d_skip, norm_g,
              *, bsz, t_len, t_valid, z_col, x_col, b_col, c_col):
    inner = norm_g.shape[-1]
    heads = inner // SSM_HEAD_DIM
    gn = SSM_GROUPS * SSM_STATE
    xbc = inner + 2 * gn
    ln = SSM_CHUNK
    nc = t_len // ln
    assert t_len % ln == 0 and heads <= LANES

    def pad_heads(v):
        return jnp.pad(v.astype(F32), (0, LANES - heads)).reshape(1, LANES)

    expand = (jnp.arange(LANES)[:, None] == (jnp.arange(inner) // SSM_HEAD_DIM)[None, :]).astype(BF16)
    d_exp = jnp.repeat(d_skip.astype(F32), SSM_HEAD_DIM).reshape(1, inner)
    const = lambda shape: pl.BlockSpec(shape, lambda b, t: (0,) * len(shape))
    rows = lambda w, j: pl.BlockSpec((ln, w), lambda b, t: (b * nc + t, j))
    return pl.pallas_call(
        functools.partial(_ssd_kernel, t_valid=t_valid),
        grid=(bsz, nc),
        in_specs=[
            rows(inner, z_col), rows(inner, x_col), rows(gn, b_col), rows(gn, c_col),
            rows(LANES, 0),
            pl.BlockSpec((1, SSM_CARRY, xbc), lambda b, t: (b, 0, 0)),
            const((SSM_CONV_K, xbc)), const((1, xbc)),
            const((1, LANES)), const((1, LANES)),
            const((1, inner)), const((1, inner)),
            const((3 * LANES, inner)), const((inner, LANES)),
            pl.BlockSpec((1, inner, SSM_STATE), lambda b, t: (b, 0, 0)),
        ],
        out_specs=[rows(inner, 0), pl.BlockSpec((1, inner, SSM_STATE), lambda b, t: (b, 0, 0))],
        out_shape=[jax.ShapeDtypeStruct((bsz * t_len, inner), BF16),
                   jax.ShapeDtypeStruct((bsz, inner, SSM_STATE), F32)],
        scratch_shapes=[pltpu.VMEM((SSM_CARRY + ln, xbc), F32), pltpu.VMEM((inner, SSM_STATE), F32)],
        compiler_params=_cparams(2),
        name="ssd_mixer",
    )(proj, proj, proj, proj, dt_raw, buf8, conv_w, conv_b.reshape(1, xbc), pad_heads(dt_bias),
      pad_heads(a_log), d_exp, norm_g.reshape(1, inner), jnp.tile(expand, (3, 1)), expand.T, h0)


def _kv_layout_kernel(*refs, n_layers):
    ins = refs[:2 * n_layers]
    ko_ref, vo_ref = refs[2 * n_layers:]
    hw = ko_ref.shape[-1]
    for j in range(n_layers):
        @pl.when(pl.program_id(0) == j)
        def _(j=j):
            for h in range(DA_HEADS):
                ko_ref[0, 0, :, h, :] = ins[2 * j][:, h * hw:(h + 1) * hw]
                vo_ref[0, 0, :, h, :] = ins[2 * j + 1][:, h * hw:(h + 1) * hw]


def kv_layout(projs, *, bsz, t_len, tt, k_col, v_col):
    n_layers = len(projs)
    hw = 2 * DA_HEAD_DIM
    width = DA_HEADS * hw
    nt = t_len // tt
    assert t_len % tt == 0

    def src(j, col):
        return pl.BlockSpec((tt, width), lambda l, b, t: (jnp.where(l == j, b * nt + t, 0), col))

    out_spec = pl.BlockSpec((1, 1, tt, DA_HEADS, hw), lambda l, b, t: (l, b, t, 0, 0))
    out_shape = jax.ShapeDtypeStruct((n_layers, bsz, t_len, DA_HEADS, hw), F32)
    in_specs, args = [], []
    for j, p in enumerate(projs):
        in_specs += [src(j, k_col), src(j, v_col)]
        args += [p, p]
    return pl.pallas_call(
        functools.partial(_kv_layout_kernel, n_layers=n_layers),
        grid=(n_layers, bsz, nt),
        in_specs=in_specs,
        out_specs=[out_spec, out_spec],
        out_shape=[out_shape, out_shape],
        compiler_params=_cparams(3),
        name="kv_layout",
    )(*args)


def _tail_rows(buf, rows, cols, t_valid, n):
    take = min(n, t_valid)
    new = rows[:, t_valid - take:t_valid, cols]
    if take == n:
        return new
    return jnp.concatenate([buf[:, buf.shape[1] - (n - take):], new], axis=1)


def _pad_front(buf, rows):
    return jnp.pad(buf, ((0, 0), (rows - buf.shape[1], 0), (0, 0)))


def _trunk(x, pemb, wts, states, *, bsz, t_len, t_valid, tm, tt, paged):
    depth = pemb.shape[0]
    d_model = x.shape[1]
    pool_w = d_model // 2
    da_w = DA_HEADS * 2 * DA_HEAD_DIM
    hw = 2 * DA_HEAD_DIM
    conv_w = d_model // 2
    inner = d_model // 2
    gn = SSM_GROUPS * SSM_STATE
    xbc_w = inner + 2 * gn
    tn = 1024
    tm2 = min(2 * tm, bsz * t_len)
    pos0 = 0 if paged is None else paged[2].shape[1] * paged[0].shape[2]
    h = x
    outs = dict(k=[], v=[], pool=[], conf=[], mconv=[], ssm=[])
    even_projs = []
    q_col = pool_w // hw

    def pad_seq_rows(a, t_new):
        return jnp.pad(a.reshape(bsz, t_len, -1),
                       ((0, 0), (0, t_new - t_len), (0, 0))).reshape(bsz * t_new, -1)

    for i in range(depth):
        j = i // 2
        if i % 2 == 0:
            proj = norm_matmul(h, wts["g_mix"][i], wts["w_in_even"], j, tm=tm, tn=tn)
            rows = proj.reshape(bsz, t_len, -1)
            pool_buf = states["pool"][j]
            outs["pool"].append(_tail_rows(pool_buf, rows, slice(0, pool_w), t_valid, POOL_BUF))
            z_pool = pool_mixer(proj, _pad_front(pool_buf, POOL_CARRY), wts["pool_w"], j,
                                wts["pool_scale"][j], bsz=bsz, t_len=t_len, tt=tt, pos0=pos0)
            lam_init = 0.8 - 0.6 * math.exp(-0.3 * i)
            lam_vecs = jnp.stack([wts["lambda_q1"][j], wts["lambda_k1"][j],
                                  wts["lambda_q2"][j], wts["lambda_k2"][j]]).astype(F32)
            if paged is None:
                even_projs.append(proj)
                attn = attention_prompt(
                    proj, lam_vecs, wts["subln_g"][j], bsz=bsz, t_len=t_len, q_col=q_col,
                    k_col=q_col + DA_HEADS, v_col=q_col + 2 * DA_HEADS, tq=512, lam_init=lam_init)
            else:
                cache_k, cache_v, page_table = paged
                page = cache_k.shape[2]
                k_seq = rows[:, :t_valid, pool_w + da_w:pool_w + 2 * da_w].reshape(
                    bsz, t_valid, DA_HEADS, hw)
                v_seq = rows[:, :t_valid, pool_w + 2 * da_w:pool_w + 3 * da_w].reshape(
                    bsz, t_valid, DA_HEADS, hw)
                outs["k"].append(k_seq)
                outs["v"].append(v_seq)
                new_page = lambda r: jnp.pad(r, ((0, 0), (0, page - t_valid), (0, 0), (0, 0)))
                attn = attention_sample(
                    proj, cache_k, cache_v, j, page_table, new_page(k_seq), new_page(v_seq),
                    lam_vecs, wts["subln_g"][j], bsz=bsz, tpad=t_len, q_col=pool_w // da_w,
                    n_pages=8, lam_init=lam_init)
            h = matmul2_residual(z_pool, attn, wts["w_out_even"], j, h, tm=tm2, tn=tn // 2)
        else:
            x0 = 2 * conv_w + inner
            proj = norm_matmul(h, wts["g_mix"][i], wts["w_in_odd"], j, tm=tm, tn=tn,
                               n_cols=x0 + xbc_w)
            dt_raw = norm_matmul(h, wts["g_mix"][i], wts["w_dt"], j, tm=tm, tn=LANES)
            rows = proj.reshape(bsz, t_len, -1)
            conf_buf = states["conf"][j]
            zc, glu = conformer_mixer(
                proj, _pad_front(conf_buf, CONV_CARRY), wts["conf_dw_w"][j], wts["conf_dw_b"][j],
                wts["conf_ln_g"][j], wts["conf_ln_b"][j], wts["conf_pw_w"], j, wts["conf_pw_b"][j],
                bsz=bsz, t_len=t_len, tt=tt)
            outs["conf"].append(_tail_rows(conf_buf, glu.reshape(bsz, t_len, conv_w),
                                           slice(None), t_valid, CONV_K - 1))
            mconv_buf = states["mconv"][j]
            outs["mconv"].append(_tail_rows(mconv_buf, rows, slice(x0, x0 + xbc_w), t_valid,
                                            SSM_CONV_K - 1))
            t_ssd = -(-t_len // SSM_CHUNK) * SSM_CHUNK
            proj_ssd = proj
            if t_ssd != t_len:
                proj_ssd = pad_seq_rows(proj, t_ssd)
                dt_raw = pad_seq_rows(dt_raw, t_ssd)
            y, h_new = ssd_mixer(
                proj_ssd, dt_raw, _pad_front(mconv_buf, SSM_CARRY), states["ssm"][j],
                wts["ssm_conv_w"][j], wts["ssm_conv_b"][j], wts["ssm_dt_bias"][j],
                wts["ssm_A_log"][j], wts["ssm_D"][j], wts["ssm_norm_g"][j],
                bsz=bsz, t_len=t_ssd, t_valid=min(t_valid, SSM_CHUNK),
                z_col=2 * conv_w // inner, x_col=x0 // inner, b_col=(x0 + inner) // gn,
                c_col=(x0 + inner + gn) // gn)
            if t_ssd != t_len:
                y = y.reshape(bsz, t_ssd, inner)[:, :t_len].reshape(bsz * t_len, inner)
            outs["ssm"].append(h_new)
            h = matmul2_residual(zc, y, wts["w_out_odd"], j, h, tm=tm2, tn=tn // 2)
        hidden = norm_matmul(h, wts["g_mlp"][i], wts["w_up"], i, tm=tm, tn=tn, act="relu2",
                             out_dtype=BF16)
        h = matmul_k_residual(hidden, wts["w_down"], i, h, tm=tm2, tn=tn, tk=2048)
        h = ple_update(h, wts["g_ple"][i], wts["w_ple_gate"], pemb, wts["w_ple_proj"], i,
                       tm=tm, tn=tn // 2)
    if paged is None:
        outs["k"], outs["v"] = kv_layout(even_projs, bsz=bsz, t_len=t_len, tt=tt,
                                         k_col=pool_w // da_w + 1, v_col=pool_w // da_w + 2)
    else:
        outs["k"], outs["v"] = jnp.stack(outs["k"]), jnp.stack(outs["v"])
    y = rmsnorm(h, wts["g_final"], tm=min(256, bsz * t_len))
    return y, outs


def kernel(x_prompt, x_sample, p_prompt, p_sample, cache_k, cache_v, page_table, state_pool, state_conf_conv, state_ssm_conv, state_ssm, g_mix, g_mlp, g_ple, g_final, w_in_even, pool_w, pool_scale, lambda_q1, lambda_k1, lambda_q2, lambda_k2, subln_g, w_out_even, w_in_odd, conf_dw_w, conf_dw_b, conf_ln_g, conf_ln_b, conf_pw_w, conf_pw_b, ssm_conv_w, ssm_conv_b, ssm_dt_bias, ssm_A_log, ssm_D, ssm_norm_g, w_out_odd, w_up, w_down, w_ple_proj, w_ple_gate):
    bp, tp, d_model = x_prompt.shape
    bs, ts, _ = x_sample.shape
    depth = p_prompt.shape[0]
    n_even, n_odd = (depth + 1) // 2, depth // 2
    inner = ssm_norm_g.shape[-1]
    n_main = w_in_odd.shape[-1] - inner // SSM_HEAD_DIM
    w_dt = jnp.pad(w_in_odd[:, :, n_main:].astype(BF16),
                   ((0, 0), (0, 0), (0, LANES - (w_in_odd.shape[-1] - n_main))))
    wts = dict(
        w_dt=w_dt,
        g_mix=g_mix, g_mlp=g_mlp, g_ple=g_ple, g_final=g_final,
        w_in_even=w_in_even.astype(BF16), pool_w=pool_w.astype(BF16), pool_scale=pool_scale,
        lambda_q1=lambda_q1, lambda_k1=lambda_k1, lambda_q2=lambda_q2, lambda_k2=lambda_k2,
        subln_g=subln_g, w_out_even=w_out_even.astype(BF16),
        w_in_odd=w_in_odd[:, :, :n_main].astype(BF16),
        conf_dw_w=conf_dw_w, conf_dw_b=conf_dw_b, conf_ln_g=conf_ln_g, conf_ln_b=conf_ln_b,
        conf_pw_w=conf_pw_w.astype(BF16), conf_pw_b=conf_pw_b, ssm_conv_w=ssm_conv_w,
        ssm_conv_b=ssm_conv_b, ssm_dt_bias=ssm_dt_bias, ssm_A_log=ssm_A_log, ssm_D=ssm_D,
        ssm_norm_g=ssm_norm_g, w_out_odd=w_out_odd.astype(BF16), w_up=w_up.astype(BF16),
        w_down=w_down.astype(BF16), w_ple_proj=w_ple_proj.astype(BF16),
        w_ple_gate=w_ple_gate.astype(BF16))
    dtype = x_prompt.dtype

    zero_states = dict(
        pool=jnp.zeros((n_even, bp, POOL_BUF, pool_scale.shape[-1]), dtype),
        conf=jnp.zeros((n_odd, bp, CONV_K - 1, conf_dw_b.shape[-1]), dtype),
        mconv=jnp.zeros((n_odd, bp, SSM_CONV_K - 1, ssm_conv_b.shape[-1]), dtype),
        ssm=jnp.zeros((n_odd, bp, inner, SSM_STATE), F32))
    y_p, o_p = _trunk(
        x_prompt.reshape(bp * tp, d_model), p_prompt.reshape(depth, bp * tp, -1).astype(BF16), wts,
        zero_states, bsz=bp, t_len=tp, t_valid=tp, tm=512, tt=256, paged=None)

    t_pad = SAMPLE_T_PAD
    pad_t = lambda a, axis: jnp.pad(a, [(0, t_pad - ts) if d == axis else (0, 0) for d in range(a.ndim)])
    sample_states = dict(pool=state_pool, conf=state_conf_conv, mconv=state_ssm_conv,
                         ssm=state_ssm.astype(F32).reshape(n_odd, bs, inner, SSM_STATE))
    y_s, o_s = _trunk(
        pad_t(x_sample, 1).reshape(bs * t_pad, d_model),
        pad_t(p_sample, 2).reshape(depth, bs * t_pad, -1).astype(BF16), wts, sample_states,
        bsz=bs, t_len=t_pad, t_valid=ts, tm=bs * t_pad, tt=t_pad,
        paged=(cache_k, cache_v, page_table))

    def ssm_out(states, bsz):
        return jnp.stack(states).reshape(n_odd, bsz, inner // SSM_HEAD_DIM, SSM_HEAD_DIM,
                                         SSM_STATE).astype(state_ssm.dtype)

    return (y_p.reshape(bp, tp, d_model), y_s.reshape(bs, t_pad, d_model)[:, :ts],
            o_p["k"], o_p["v"], o_s["k"], o_s["v"],
            jnp.stack(o_p["pool"]), jnp.stack(o_s["pool"]),
            jnp.stack(o_p["conf"]), jnp.stack(o_s["conf"]),
            jnp.stack(o_p["mconv"]), jnp.stack(o_s["mconv"]),
            ssm_out(o_p["ssm"], bp), ssm_out(o_s["ssm"], bs))
```

```python
import functools
import math
from typing import NamedTuple

import jax
import jax.numpy as jnp
from jax import lax
from jax.experimental import pallas as pl
from jax.experimental.pallas import tpu as pltpu

F32 = jnp.float32
BF16 = jnp.bfloat16
EPS = 1e-6
NEG_INF = -1e30

V7X_VMEM_BYTES = 64 * 1024 * 1024
VMEM_LIMIT = V7X_VMEM_BYTES - 4 * 1024 * 1024

LANES = 128
SUBLANES = 8
BF16_SUBLANES = 16
POOL_WINDOWS = (2, 4, 8, 16)
POOL_BUF = max(POOL_WINDOWS) - 1
POOL_CARRY = 16
CONV_K = 31
CONV_CARRY = 32
SSM_CONV_K = 4
SSM_CARRY = 8
SSM_CHUNK = 128
DA_HEADS = 8
DA_HEAD_DIM = 128
SSM_HEAD_DIM = 64
SSM_GROUPS = 8
SSM_STATE = 128
SAMPLE_T_PAD = 16


def _cparams(n_axes):
    return pltpu.CompilerParams(
        dimension_semantics=("arbitrary",) * n_axes, vmem_limit_bytes=VMEM_LIMIT)


def _rms_rows(x, g):
    ms = jnp.mean(x * x, axis=-1, keepdims=True)
    return x * lax.rsqrt(ms + EPS) * g


def _sigmoid(x):
    return 0.5 * jnp.tanh(0.5 * x) + 0.5


def _normalise_row_tile(x_hbm, g_ref, xbuf_ref, sem_ref, xn_ref):
    i = pl.program_id(0)
    tm = xn_ref.shape[0]

    def row_copy(tile, slot):
        return pltpu.make_async_copy(x_hbm.at[pl.ds(tile * tm, tm), :], xbuf_ref.at[slot],
                                     sem_ref.at[slot])

    @pl.when(pl.program_id(1) == 0)
    def _():
        slot = lax.rem(i, 2)

        @pl.when(i == 0)
        def _():
            row_copy(0, 0).start()

        row_copy(i, slot).wait()

        @pl.when(i + 1 < pl.num_programs(0))
        def _():
            row_copy(i + 1, 1 - slot).start()

        xn_ref[...] = _rms_rows(xbuf_ref[slot], g_ref[...]).astype(BF16)


def _row_tile_scratch(tm, k):
    return [pltpu.VMEM((2, tm, k), F32), pltpu.SemaphoreType.DMA((2,)), pltpu.VMEM((tm, k), BF16)]


class WeightCast(NamedTuple):
    src: jax.Array
    layer: int
    n_cols: int


def _cast_specs(cast, grid):
    n_steps = math.prod(grid)
    rows = cast.src.shape[1]
    r_blk = max(BF16_SUBLANES, 1 << math.ceil(math.log2(rows / n_steps)))
    assert rows % r_blk == 0
    n_tiles = rows // r_blk

    def tile(*g):
        step = g[0]
        for idx, extent in zip(g[1:], grid[1:]):
            step = step * extent + idx
        return jnp.minimum(step, n_tiles - 1)

    return (pl.BlockSpec((None, r_blk, cast.n_cols), lambda *g: (cast.layer, tile(*g), 0)),
            pl.BlockSpec((r_blk, cast.n_cols), lambda *g: (tile(*g), 0)),
            jax.ShapeDtypeStruct((rows, cast.n_cols), BF16))


def _dense_call(kernel_fn, grid, in_specs, args, out_spec, out_shape, scratch, cast, name):
    out_specs, out_shapes = [out_spec], [out_shape]
    if cast is not None:
        c_in, c_out, c_shape = _cast_specs(cast, grid)
        in_specs, args = in_specs + [c_in], args + [cast.src]
        out_specs, out_shapes = out_specs + [c_out], out_shapes + [c_shape]
    res = pl.pallas_call(
        functools.partial(kernel_fn, n_in=len(in_specs), has_cast=cast is not None),
        grid=grid, in_specs=in_specs, out_specs=out_specs, out_shape=out_shapes,
        scratch_shapes=scratch, compiler_params=_cparams(len(grid)), name=name)(*args)
    return res if cast is not None else res[0]


def _split_refs(refs, n_in, has_cast):
    n_out = 2 if has_cast else 1
    ins, outs, scratch = refs[:n_in], refs[n_in:n_in + n_out], refs[n_in + n_out:]
    if has_cast:
        outs[1][...] = ins[-1][...].astype(BF16)
        ins = ins[:-1]
    return ins, outs[0], scratch


def _norm_mm_kernel(*refs, n_in, has_cast, act):
    (x_hbm, g_ref, w_ref), o_ref, (xbuf_ref, sem_ref, xn_ref) = _split_refs(refs, n_in, has_cast)
    _normalise_row_tile(x_hbm, g_ref, xbuf_ref, sem_ref, xn_ref)
    acc = jnp.dot(xn_ref[...], w_ref[...], preferred_element_type=F32)
    if act == "relu2":
        acc = jnp.square(jnp.maximum(acc, 0.0))
    o_ref[...] = acc.astype(o_ref.dtype)


def norm_matmul(x, g, w, *, tm, tn, act=None, out_dtype=F32, cast=None):
    m, k = x.shape
    n = w.shape[1]
    assert m % tm == 0 and n % tn == 0
    return _dense_call(
        functools.partial(_norm_mm_kernel, act=act), (m // tm, n // tn),
        [pl.BlockSpec(memory_space=pl.ANY),
         pl.BlockSpec((1, k), lambda i, j: (0, 0)),
         pl.BlockSpec((k, tn), lambda i, j: (0, j))],
        [x, g.reshape(1, k), w],
        pl.BlockSpec((tm, tn), lambda i, j: (i, j)), jax.ShapeDtypeStruct((m, n), out_dtype),
        _row_tile_scratch(tm, k), cast, "norm_matmul")


def _mm2_res_kernel(*refs, n_in, has_cast):
    (a1_ref, a2_ref, w_ref, r_ref), o_ref, _ = _split_refs(refs, n_in, has_cast)
    k1 = a1_ref.shape[1]
    acc = jnp.dot(a1_ref[...], w_ref[:k1, :], preferred_element_type=F32)
    acc += jnp.dot(a2_ref[...], w_ref[k1:, :], preferred_element_type=F32)
    o_ref[...] = r_ref[...] + acc


def matmul2_residual(a1, a2, w, res, *, tm, tn, cast=None):
    m, k1 = a1.shape
    k2 = a2.shape[1]
    n = w.shape[1]
    assert m % tm == 0 and n % tn == 0 and w.shape[0] == k1 + k2
    return _dense_call(
        _mm2_res_kernel, (m // tm, n // tn),
        [pl.BlockSpec((tm, k1), lambda i, j: (i, 0)),
         pl.BlockSpec((tm, k2), lambda i, j: (i, 0)),
         pl.BlockSpec((k1 + k2, tn), lambda i, j: (0, j)),
         pl.BlockSpec((tm, tn), lambda i, j: (i, j))],
        [a1, a2, w, res],
        pl.BlockSpec((tm, tn), lambda i, j: (i, j)), jax.ShapeDtypeStruct((m, n), F32),
        [], cast, "matmul2_residual")


def _mmk_res_kernel(*refs, n_in, has_cast):
    (a_ref, w_ref, r_ref), o_ref, _ = _split_refs(refs, n_in, has_cast)

    @pl.when(pl.program_id(2) == 0)
    def _():
        o_ref[...] = r_ref[...]

    o_ref[...] += jnp.dot(a_ref[...], w_ref[...], preferred_element_type=F32)


def matmul_k_residual(a, w, res, *, tm, tn, tk, cast=None):
    m, k = a.shape
    n = w.shape[1]
    assert m % tm == 0 and n % tn == 0 and k % tk == 0
    return _dense_call(
        _mmk_res_kernel, (m // tm, n // tn, k // tk),
        [pl.BlockSpec((tm, tk), lambda i, j, l: (i, l)),
         pl.BlockSpec((tk, tn), lambda i, j, l: (l, j)),
         pl.BlockSpec((tm, tn), lambda i, j, l: (i, j))],
        [a, w, res],
        pl.BlockSpec((tm, tn), lambda i, j, l: (i, j)), jax.ShapeDtypeStruct((m, n), F32),
        [], cast, "matmul_k_residual")


def _ple_kernel(*refs, n_in, has_cast):
    ((x_hbm, g_ref, wg_ref, pe_ref, wp_ref, r_ref), o_ref,
     (xbuf_ref, sem_ref, xn_ref)) = _split_refs(refs, n_in, has_cast)
    _normalise_row_tile(x_hbm, g_ref, xbuf_ref, sem_ref, xn_ref)
    gate = jax.nn.sigmoid(jnp.dot(xn_ref[...], wg_ref[...], preferred_element_type=F32))
    proj = jnp.dot(pe_ref[...], wp_ref[...], preferred_element_type=F32)
    o_ref[...] = r_ref[...] + gate * proj


def ple_update(h, g, wg, pe, wp, layer, *, tm, tn, cast=None):
    m, k = h.shape
    n = wg.shape[1]
    kp = pe.shape[2]
    assert m % tm == 0 and n % tn == 0
    return _dense_call(
        _ple_kernel, (m // tm, n // tn),
        [pl.BlockSpec(memory_space=pl.ANY),
         pl.BlockSpec((1, k), lambda i, j: (0, 0)),
         pl.BlockSpec((k, tn), lambda i, j: (0, j)),
         pl.BlockSpec((None, tm, kp), lambda i, j: (layer, i, 0)),
         pl.BlockSpec((None, kp, tn), lambda i, j: (layer, 0, j)),
         pl.BlockSpec((tm, tn), lambda i, j: (i, j))],
        [h, g.reshape(1, k), wg, pe, wp, h],
        pl.BlockSpec((tm, tn), lambda i, j: (i, j)), jax.ShapeDtypeStruct((m, n), F32),
        _row_tile_scratch(tm, k), cast, "ple_update")


def _rmsnorm_kernel(x_ref, g_ref, o_ref):
    o_ref[...] = _rms_rows(x_ref[...], g_ref[...])


def rmsnorm(x, g, *, tm):
    m, k = x.shape
    assert m % tm == 0
    return pl.pallas_call(
        _rmsnorm_kernel,
        grid=(m // tm,),
        in_specs=[pl.BlockSpec((tm, k), lambda i: (i, 0)), pl.BlockSpec((1, k), lambda i: (0, 0))],
        out_specs=pl.BlockSpec((tm, k), lambda i: (i, 0)),
        out_shape=jax.ShapeDtypeStruct((m, k), F32),
        compiler_params=_cparams(1),
        name="final_rmsnorm",
    )(x, g.reshape(1, k))


def _pool_kernel(u_ref, buf_ref, w_ref, scale_ref, z_ref, full_ref, *, tt, pos0):
    t = pl.program_id(1)

    @pl.when(t == 0)
    def _():
        full_ref[0:POOL_CARRY, :] = buf_ref[0]

    @pl.when(t != 0)
    def _():
        full_ref[0:POOL_CARRY, :] = full_ref[tt:tt + POOL_CARRY, :]

    full_ref[POOL_CARRY:POOL_CARRY + tt, :] = u_ref[...]
    gc = w_ref.shape[1]
    pos = pos0 + t * tt + lax.broadcasted_iota(jnp.int32, (tt, 1), 0)
    for g, win in enumerate(POOL_WINDOWS):
        cols = slice(g * gc, (g + 1) * gc)
        cur = full_ref[POOL_CARRY:POOL_CARRY + tt, cols]
        tot = cur
        for i in range(1, win):
            tot = tot + full_ref[POOL_CARRY - i:POOL_CARRY - i + tt, cols]
        cnt = jnp.minimum(pos + 1, win).astype(F32)
        d = tot / cnt - cur
        z = jnp.dot(d.astype(BF16), w_ref[g], preferred_element_type=F32)
        z_ref[:, cols] = (z * scale_ref[:, cols]).astype(z_ref.dtype)


def pool_mixer(proj, buf16, w, layer, scale, *, bsz, t_len, tt, pos0):
    c = scale.shape[-1]
    nt = t_len // tt
    assert t_len % tt == 0 and (nt == 1 or tt >= POOL_CARRY)
    return pl.pallas_call(
        functools.partial(_pool_kernel, tt=tt, pos0=pos0),
        grid=(bsz, nt),
        in_specs=[
            pl.BlockSpec((tt, c), lambda b, t: (b * nt + t, 0)),
            pl.BlockSpec((1, POOL_CARRY, c), lambda b, t: (b, 0, 0)),
            pl.BlockSpec((None,) + w.shape[1:], lambda b, t: (layer, 0, 0, 0)),
            pl.BlockSpec((1, c), lambda b, t: (0, 0)),
        ],
        out_specs=pl.BlockSpec((tt, c), lambda b, t: (b * nt + t, 0)),
        out_shape=jax.ShapeDtypeStruct((bsz * t_len, c), BF16),
        scratch_shapes=[pltpu.VMEM((POOL_CARRY + tt, c), F32)],
        compiler_params=_cparams(2),
        name="pool_mixer",
    )(proj, buf16, w, scale.reshape(1, c))


def _lambda_value(lam_ref, lam_init):
    v = lam_ref[...]
    s1 = jnp.sum(v[0:1] * v[1:2], axis=-1, keepdims=True)
    s2 = jnp.sum(v[2:3] * v[3:4], axis=-1, keepdims=True)
    return jnp.exp(s1) - jnp.exp(s2) + lam_init


def _head_slopes(head_idx_f32):
    return jnp.exp2((-8.0 / DA_HEADS) * (head_idx_f32 + 1.0))


def _subln(o, g, lam_init):
    ms = jnp.mean(o * o, axis=-1, keepdims=True)
    return o * lax.rsqrt(ms + EPS) * g * (1.0 - lam_init)


LOG2E = math.log2(math.e)


def _softmax_tile(s, v_bf16, m_ref, l_ref, acc_ref, idx):
    cols = s.shape[1]
    m_prev = m_ref[idx]
    m_next = jnp.maximum(m_prev, jnp.max(s, axis=-1, keepdims=True))
    alpha = jnp.exp2(m_prev - m_next)
    p = jnp.exp2(s - jnp.tile(m_next, (1, cols // LANES)))
    part = p[:, 0:LANES]
    for i in range(1, cols // LANES):
        part = part + p[:, i * LANES:(i + 1) * LANES]
    l_ref[idx] = alpha * l_ref[idx] + part
    acc_ref[idx] = jnp.tile(alpha, (1, acc_ref.shape[-1] // LANES)) * acc_ref[idx] + jnp.dot(
        p.astype(BF16), v_bf16, preferred_element_type=F32)
    m_ref[idx] = m_next


def _attn_prompt_kernel(q_ref, k_ref, v_ref, lam_ref, g_ref, o_ref, kb_ref, vb_ref, m_ref, l_ref,
                        acc_ref, *, tq, lam_init):
    hd = DA_HEAD_DIM
    h = pl.program_id(1)
    qi = pl.program_id(2)

    @pl.when(qi == 0)
    def _():
        kb_ref[...] = k_ref[...].astype(BF16)
        vb_ref[...] = v_ref[...].astype(BF16)

    slope2 = _head_slopes(jnp.full((1, 1), h, jnp.int32).astype(F32)) * LOG2E
    q = (q_ref[...] * (hd ** -0.5 * LOG2E)).astype(BF16)
    m_ref[...] = jnp.full(m_ref.shape, NEG_INF, F32)
    l_ref[...] = jnp.zeros(l_ref.shape, F32)
    acc_ref[...] = jnp.zeros(acc_ref.shape, F32)
    k_local = lax.broadcasted_iota(jnp.int32, (1, tq), 1)

    def tile(ki, diagonal):
        start = pl.multiple_of(ki * tq, tq)
        k = kb_ref[pl.ds(start, tq), :]
        v = vb_ref[pl.ds(start, tq), :]
        col_bias = slope2 * (k_local + ki * tq).astype(F32)
        for c in range(2):
            s = lax.dot_general(q[:, c * hd:(c + 1) * hd], k[:, c * hd:(c + 1) * hd],
                                (((1,), (1,)), ((), ())), preferred_element_type=F32) + col_bias
            if diagonal:
                keep = (lax.broadcasted_iota(jnp.int32, (tq, tq), 0)
                        >= lax.broadcasted_iota(jnp.int32, (tq, tq), 1))
                s = jnp.where(keep, s, NEG_INF)
            _softmax_tile(s, v, m_ref, l_ref, acc_ref, c)

    def body(ki, carry):
        tile(ki, False)
        return carry

    lax.fori_loop(0, qi, body, 0)
    tile(qi, True)
    lam = _lambda_value(lam_ref, lam_init)
    o = (acc_ref[0] / jnp.sum(l_ref[0], axis=-1, keepdims=True)
         - lam * (acc_ref[1] / jnp.sum(l_ref[1], axis=-1, keepdims=True)))
    o_ref[...] = _subln(o, g_ref[...], lam_init).astype(o_ref.dtype)


def attention_prompt(proj, lam_vecs, subln_g, *, bsz, t_len, q_col, k_col, v_col, tq, lam_init):
    hw = 2 * DA_HEAD_DIM
    nq = t_len // tq
    assert t_len % tq == 0
    return pl.pallas_call(
        functools.partial(_attn_prompt_kernel, tq=tq, lam_init=lam_init),
        grid=(bsz, DA_HEADS, nq),
        in_specs=[
            pl.BlockSpec((tq, hw), lambda b, h, i: (b * nq + i, q_col + h)),
            pl.BlockSpec((t_len, hw), lambda b, h, i: (b, k_col + h)),
            pl.BlockSpec((t_len, hw), lambda b, h, i: (b, v_col + h)),
            pl.BlockSpec((4, DA_HEAD_DIM), lambda b, h, i: (0, 0)),
            pl.BlockSpec((1, hw), lambda b, h, i: (0, 0)),
        ],
        out_specs=pl.BlockSpec((tq, hw), lambda b, h, i: (b * nq + i, h)),
        out_shape=jax.ShapeDtypeStruct((bsz * t_len, DA_HEADS * hw), BF16),
        scratch_shapes=[pltpu.VMEM((t_len, hw), BF16), pltpu.VMEM((t_len, hw), BF16),
                        pltpu.VMEM((2, tq, LANES), F32), pltpu.VMEM((2, tq, LANES), F32),
                        pltpu.VMEM((2, tq, hw), F32)],
        compiler_params=_cparams(3),
        name="attention_prompt",
    )(proj, proj, proj, lam_vecs, subln_g.reshape(1, hw))


def _attn_sample_kernel(pt_ref, q_ref, *refs, n_pages, tpad, pos0, lam_init):
    del pt_ref
    k_refs = refs[:n_pages]
    v_refs = refs[n_pages:2 * n_pages]
    kn_ref, vn_ref, lam_ref, g_ref, o_ref, qall_ref, bias_ref, m_ref, l_ref, acc_ref = refs[2 * n_pages:]
    hd = DA_HEAD_DIM
    nh = DA_HEADS
    page = k_refs[0].shape[2]
    nrow = nh * 2 * tpad
    ncol = page * nh
    sidx = pl.program_id(1)
    sh_row = int(math.log2(2 * tpad))
    sh_col = int(math.log2(nh))
    assert 1 << sh_row == 2 * tpad and 1 << sh_col == nh

    def rel_and_match():
        row = lax.broadcasted_iota(jnp.int32, (nrow, ncol), 0)
        col = lax.broadcasted_iota(jnp.int32, (nrow, ncol), 1)
        rel = (jnp.right_shift(col, sh_col) - jnp.bitwise_and(row, tpad - 1)).astype(F32)
        match = jnp.right_shift(row, sh_row) == jnp.bitwise_and(col, nh - 1)
        return rel, match

    def row_slopes(shape):
        row = lax.broadcasted_iota(jnp.int32, shape, 0)
        return _head_slopes(jnp.right_shift(row, sh_row).astype(F32)) * LOG2E

    @pl.when(sidx == 0)
    def _():
        q = q_ref[...] * (hd ** -0.5 * LOG2E)
        lane = lax.broadcasted_iota(jnp.int32, (tpad, 2 * hd), 1)
        for h in range(nh):
            qh = q[:, h * 2 * hd:(h + 1) * 2 * hd]
            qall_ref[h * 2 * tpad:h * 2 * tpad + tpad, :] = jnp.where(lane < hd, qh, 0.0).astype(BF16)
            qall_ref[h * 2 * tpad + tpad:(h + 1) * 2 * tpad, :] = jnp.where(lane >= hd, qh, 0.0).astype(BF16)
        m_ref[...] = jnp.full(m_ref.shape, NEG_INF, F32)
        l_ref[...] = jnp.zeros(l_ref.shape, F32)
        acc_ref[...] = jnp.zeros(acc_ref.shape, F32)
        rel, match = rel_and_match()
        bias_ref[...] = jnp.where(match, row_slopes((nrow, ncol)) * rel, NEG_INF)

    def scores(k_page):
        kk = k_page.reshape(ncol, 2 * hd).astype(BF16)
        return lax.dot_general(qall_ref[...], kk, (((1,), (1,)), ((), ())), preferred_element_type=F32)

    slope_col = row_slopes((nrow, 1))
    for i in range(n_pages):
        base = ((sidx * n_pages + i) * page - pos0).astype(F32)
        s = scores(k_refs[i][0, 0]) + bias_ref[...] + slope_col * base
        vv = v_refs[i][0, 0].reshape(ncol, 2 * hd).astype(BF16)
        _softmax_tile(s, vv, m_ref, l_ref, acc_ref, 0)

    @pl.when(sidx == pl.num_programs(1) - 1)
    def _():
        rel, _ = rel_and_match()
        s = scores(kn_ref[0]) + jnp.where(rel <= 0.0, bias_ref[...], NEG_INF)
        vv = vn_ref[0].reshape(ncol, 2 * hd).astype(BF16)
        _softmax_tile(s, vv, m_ref, l_ref, acc_ref, 0)
        lam = _lambda_value(lam_ref, lam_init)
        a = acc_ref[0] / jnp.sum(l_ref[0], axis=-1, keepdims=True)
        for h in range(nh):
            r0 = h * 2 * tpad
            o = _subln(a[r0:r0 + tpad] - lam * a[r0 + tpad:r0 + 2 * tpad], g_ref[...], lam_init)
            o_ref[:, h * 2 * hd:(h + 1) * 2 * hd] = o.astype(o_ref.dtype)


def attention_sample(proj, cache_k, cache_v, layer, page_table, k_new, v_new, lam_vecs, subln_g,
                     *, bsz, tpad, q_col, n_pages, lam_init):
    hw = 2 * DA_HEAD_DIM
    qw = DA_HEADS * hw
    page = cache_k.shape[2]
    n_past_pages = page_table.shape[1]
    assert n_past_pages % n_pages == 0
    pos0 = n_past_pages * page
    nrow = DA_HEADS * 2 * tpad
    page_block = (1, 1, page, DA_HEADS, hw)

    def page_spec(i):
        return pl.BlockSpec(page_block, lambda b, s, pt: (layer, pt[b, s * n_pages + i], 0, 0, 0))

    new_spec = pl.BlockSpec((1, page, DA_HEADS, hw), lambda b, s, pt: (b, 0, 0, 0))
    grid_spec = pltpu.PrefetchScalarGridSpec(
        num_scalar_prefetch=1,
        grid=(bsz, n_past_pages // n_pages),
        in_specs=[pl.BlockSpec((tpad, qw), lambda b, s, pt: (b, q_col))]
        + [page_spec(i) for i in range(n_pages)] * 2
        + [new_spec, new_spec,
           pl.BlockSpec((4, DA_HEAD_DIM), lambda b, s, pt: (0, 0)),
           pl.BlockSpec((1, hw), lambda b, s, pt: (0, 0))],
        out_specs=pl.BlockSpec((tpad, qw), lambda b, s, pt: (b, 0)),
        scratch_shapes=[pltpu.VMEM((nrow, hw), BF16), pltpu.VMEM((nrow, page * DA_HEADS), F32),
                        pltpu.VMEM((1, nrow, LANES), F32), pltpu.VMEM((1, nrow, LANES), F32),
                        pltpu.VMEM((1, nrow, hw), F32)],
    )
    return pl.pallas_call(
        functools.partial(_attn_sample_kernel, n_pages=n_pages, tpad=tpad, pos0=pos0, lam_init=lam_init),
        grid_spec=grid_spec,
        out_shape=jax.ShapeDtypeStruct((bsz * tpad, qw), BF16),
        compiler_params=_cparams(2),
        name="attention_sample",
    )(page_table, proj, *([cache_k] * n_pages), *([cache_v] * n_pages), k_new, v_new,
      lam_vecs, subln_g.reshape(1, hw))


def _conformer_kernel(a_ref, gate_ref, buf_ref, dww_ref, dwb_ref, lng_ref, lnb_ref, pww_ref, pwb_ref,
                      z_ref, glu_ref, full_ref, cv_ref, *, tt):
    t = pl.program_id(1)

    @pl.when(t == 0)
    def _():
        full_ref[0:CONV_CARRY, :] = buf_ref[0]

    @pl.when(t != 0)
    def _():
        full_ref[0:CONV_CARRY, :] = full_ref[tt:tt + CONV_CARRY, :]

    glu = a_ref[...] * _sigmoid(gate_ref[...])
    full_ref[CONV_CARRY:CONV_CARRY + tt, :] = glu
    glu_ref[...] = glu
    first = CONV_CARRY - (CONV_K - 1)
    rb = min(tt, 128)
    phases = {}
    for k in range(CONV_K):
        phases.setdefault((first + k) % SUBLANES, []).append(k)

    def lane_chunk(c, carry):
        col = pl.ds(pl.multiple_of(c * LANES, LANES), LANES)
        for r0 in range(0, tt, rb):
            acc = jnp.zeros((rb, LANES), F32) + dwb_ref[:, col]
            for phase, taps in phases.items():
                rows = rb if phase == 0 else rb + SUBLANES
                part = None
                for k in taps:
                    base = r0 + first + k - phase
                    term = dww_ref[k:k + 1, col] * full_ref[base:base + rows, col]
                    part = term if part is None else part + term
                acc = acc + part[phase:phase + rb]
            cv_ref[r0:r0 + rb, col] = acc
        return carry

    lax.fori_loop(0, cv_ref.shape[1] // LANES, lane_chunk, 0)
    y = cv_ref[...]
    mu = jnp.mean(y, axis=-1, keepdims=True)
    yc = y - mu
    var = jnp.mean(yc * yc, axis=-1, keepdims=True)
    y = yc * lax.rsqrt(var + EPS) * lng_ref[...] + lnb_ref[...]
    y = y * _sigmoid(y)
    z = jnp.dot(y.astype(BF16), pww_ref[...], preferred_element_type=F32) + pwb_ref[...]
    z_ref[...] = z.astype(z_ref.dtype)


def conformer_mixer(proj, buf32, dw_w, dw_b, ln_g, ln_b, pw_w, layer, pw_b, *, bsz, t_len, tt):
    c = dw_b.shape[-1]
    nt = t_len // tt
    assert t_len % tt == 0 and (nt == 1 or tt >= CONV_CARRY)
    vec = lambda: pl.BlockSpec((1, c), lambda b, t: (0, 0))
    rows = lambda j: pl.BlockSpec((tt, c), lambda b, t: (b * nt + t, j))
    return pl.pallas_call(
        functools.partial(_conformer_kernel, tt=tt),
        grid=(bsz, nt),
        in_specs=[
            rows(0), rows(1),
            pl.BlockSpec((1, CONV_CARRY, c), lambda b, t: (b, 0, 0)),
            pl.BlockSpec((CONV_K, c), lambda b, t: (0, 0)),
            vec(), vec(), vec(),
            pl.BlockSpec((None, c, c), lambda b, t: (layer, 0, 0)),
            vec(),
        ],
        out_specs=[rows(0), rows(0)],
        out_shape=[jax.ShapeDtypeStruct((bsz * t_len, c), BF16),
                   jax.ShapeDtypeStruct((bsz * t_len, c), F32)],
        scratch_shapes=[pltpu.VMEM((CONV_CARRY + tt, c), F32), pltpu.VMEM((tt, c), F32)],
        compiler_params=_cparams(2),
        name="conformer_mixer",
    )(proj, proj, buf32, dw_w, dw_b.reshape(1, c), ln_g.reshape(1, c), ln_b.reshape(1, c),
      pw_w, pw_b.reshape(1, c))


def _split3(x, axis):
    hi = x.astype(BF16)
    rest = x - hi.astype(F32)
    mid = rest.astype(BF16)
    lo = (rest - mid.astype(F32)).astype(BF16)
    return jnp.concatenate([hi, mid, lo], axis=axis)


def _select_dot_left(sel_bf16, x):
    n = x.shape[1]
    r = jnp.dot(sel_bf16, _split3(x, 1), preferred_element_type=F32)
    return r[:, 0:n] + r[:, n:2 * n] + r[:, 2 * n:3 * n]


def _ssd_kernel(z_ref, x_ref, b_ref, c_ref, dt_ref, buf_ref, cw_ref, cb_ref, dtb_ref, alog_ref,
                dexp_ref, ng_ref, e_ref, et_ref, h0_ref, y_ref, hn_ref, full_ref, st_ref,
                *, t_valid):
    t = pl.program_id(1)
    ln = x_ref.shape[0]
    inner = x_ref.shape[1]
    gn = b_ref.shape[1]
    ns = SSM_STATE
    gw = inner // SSM_GROUPS
    hpg = gw // SSM_HEAD_DIM

    @pl.when(t == 0)
    def _():
        full_ref[0:SSM_CARRY, :] = buf_ref[0]
        st_ref[...] = h0_ref[0]

    @pl.when(t != 0)
    def _():
        full_ref[0:SSM_CARRY, :] = full_ref[ln:ln + SSM_CARRY, :]

    full_ref[SSM_CARRY:SSM_CARRY + ln, 0:inner] = x_ref[...]
    full_ref[SSM_CARRY:SSM_CARRY + ln, inner:inner + gn] = b_ref[...]
    full_ref[SSM_CARRY:SSM_CARRY + ln, inner + gn:inner + 2 * gn] = c_ref[...]
    first = SSM_CARRY - (SSM_CONV_K - 1)
    xc = cb_ref[...] + cw_ref[0:1, :] * full_ref[first:first + ln, :]
    for k in range(1, SSM_CONV_K):
        xc = xc + cw_ref[k:k + 1, :] * full_ref[first + k:first + k + ln, :]
    xc = xc * _sigmoid(xc)
    xs = xc[:, 0:inner]
    bm = xc[:, inner:inner + gn].astype(BF16)
    cm = xc[:, inner + gn:inner + 2 * gn].astype(BF16)

    dt_in = dt_ref[...] + dtb_ref[...]
    dt = jnp.maximum(dt_in, 0.0) + jnp.log1p(jnp.exp(-jnp.abs(dt_in)))
    row = lax.broadcasted_iota(jnp.int32, (ln, ln), 0)
    col = lax.broadcasted_iota(jnp.int32, (ln, ln), 1)
    if t_valid < ln:
        dt = jnp.where(lax.broadcasted_iota(jnp.int32, dt.shape, 0) < t_valid, dt, 0.0)
    ad = dt * (-jnp.exp(alog_ref[...]))
    causal = row >= col
    a_cs = _select_dot_left(jnp.where(causal, 1.0, 0.0).astype(BF16), ad)
    a_cs_t = a_cs.T
    a_last = a_cs[ln - 1:ln, :]
    per_head = jnp.concatenate([dt, jnp.exp(a_last - a_cs), jnp.exp(a_cs)], axis=0)
    per_chan = jnp.dot(_split3(per_head, 1), e_ref[...], preferred_element_type=F32)
    xd = xs * per_chan[0:ln]
    xdw = (xd * per_chan[ln:2 * ln]).astype(BF16)
    xd = xd.astype(BF16)
    exp_acs = per_chan[2 * ln:3 * ln]
    last_t = jnp.broadcast_to(jnp.exp(a_cs_t[:, ln - 1:ln]), a_cs_t.shape)
    st_decay = _select_dot_left(et_ref[...], last_t)

    lane_head = lax.broadcasted_iota(jnp.int32, (ln, gw), 1) // SSM_HEAD_DIM
    for g in range(SSM_GROUPS):
        cols = slice(g * gw, (g + 1) * gw)
        scols = slice(g * ns, (g + 1) * ns)
        scores = lax.dot_general(cm[:, scols], bm[:, scols], (((1,), (1,)), ((), ())),
                                 preferred_element_type=F32)
        lhs, rhs = [], []
        for r in range(hpg):
            h = g * hpg + r
            seg = a_cs[:, h:h + 1] - a_cs_t[h:h + 1, :]
            lhs.append((scores * jnp.where(causal, jnp.exp(seg), 0.0)).astype(BF16))
            rhs.append(jnp.where(lane_head == r, xd[:, cols], jnp.zeros_like(xd[:, cols])))
        y = jnp.dot(jnp.concatenate(lhs, axis=1), jnp.concatenate(rhs, axis=0),
                    preferred_element_type=F32)
        h_prev = st_ref[cols, :]
        y = y + exp_acs[:, cols] * lax.dot_general(
            cm[:, scols], h_prev.astype(BF16), (((1,), (1,)), ((), ())), preferred_element_type=F32)
        new_states = lax.dot_general(xdw[:, cols], bm[:, scols], (((0,), (0,)), ((), ())),
                                     preferred_element_type=F32)
        st_ref[cols, :] = h_prev * st_decay[cols, :] + new_states
        y = y + dexp_ref[:, cols] * xs[:, cols]
        zg = z_ref[:, cols]
        y = y * (zg * _sigmoid(zg))
        y_ref[:, cols] = (_rms_rows(y, ng_ref[:, cols])).astype(y_ref.dtype)

    @pl.when(t == pl.num_programs(1) - 1)
    def _():
        hn_ref[0] = st_ref[...]


def ssd_mixer(proj, dt_raw, buf8, h0, conv_w, conv_b, dt_bias, a_log, d_skip, norm_g,
              *, bsz, t_len, t_valid, z_col, x_col, b_col, c_col):
    inner = norm_g.shape[-1]
    heads = inner // SSM_HEAD_DIM
    gn = SSM_GROUPS * SSM_STATE
    xbc = inner + 2 * gn
    ln = SSM_CHUNK
    nc = t_len // ln
    assert t_len % ln == 0 and heads <= LANES

    def pad_heads(v):
        return jnp.pad(v.astype(F32), (0, LANES - heads)).reshape(1, LANES)

    expand = (jnp.arange(LANES)[:, None] == (jnp.arange(inner) // SSM_HEAD_DIM)[None, :]).astype(BF16)
    d_exp = jnp.repeat(d_skip.astype(F32), SSM_HEAD_DIM).reshape(1, inner)
    const = lambda shape: pl.BlockSpec(shape, lambda b, t: (0,) * len(shape))
    rows = lambda w, j: pl.BlockSpec((ln, w), lambda b, t: (b * nc + t, j))
    return pl.pallas_call(
        functools.partial(_ssd_kernel, t_valid=t_valid),
        grid=(bsz, nc),
        in_specs=[
            rows(inner, z_col), rows(inner, x_col), rows(gn, b_col), rows(gn, c_col),
            rows(LANES, 0),
            pl.BlockSpec((1, SSM_CARRY, xbc), lambda b, t: (b, 0, 0)),
            const((SSM_CONV_K, xbc)), const((1, xbc)),
            const((1, LANES)), const((1, LANES)),
            const((1, inner)), const((1, inner)),
            const((3 * LANES, inner)), const((inner, LANES)),
            pl.BlockSpec((1, inner, SSM_STATE), lambda b, t: (b, 0, 0)),
        ],
        out_specs=[rows(inner, 0), pl.BlockSpec((1, inner, SSM_STATE), lambda b, t: (b, 0, 0))],
        out_shape=[jax.ShapeDtypeStruct((bsz * t_len, inner), BF16),
                   jax.ShapeDtypeStruct((bsz, inner, SSM_STATE), F32)],
        scratch_shapes=[pltpu.VMEM((SSM_CARRY + ln, xbc), F32), pltpu.VMEM((inner, SSM_STATE), F32)],
        compiler_params=_cparams(2),
        name="ssd_mixer",
    )(proj, proj, proj, proj, dt_raw, buf8, conv_w, conv_b.reshape(1, xbc), pad_heads(dt_bias),
      pad_heads(a_log), d_exp, norm_g.reshape(1, inner), jnp.tile(expand, (3, 1)), expand.T, h0)


def _kv_layout_kernel(*refs, n_layers):
    ins = refs[:2 * n_layers]
    ko_ref, vo_ref = refs[2 * n_layers:]
    hw = ko_ref.shape[-1]
    for j in range(n_layers):
        @pl.when(pl.program_id(0) == j)
        def _(j=j):
            for h in range(DA_HEADS):
                ko_ref[0, 0, :, h, :] = ins[2 * j][:, h * hw:(h + 1) * hw]
                vo_ref[0, 0, :, h, :] = ins[2 * j + 1][:, h * hw:(h + 1) * hw]


def kv_layout(projs, *, bsz, t_len, tt, k_col, v_col):
    n_layers = len(projs)
    hw = 2 * DA_HEAD_DIM
    width = DA_HEADS * hw
    nt = t_len // tt
    assert t_len % tt == 0

    def src(j, col):
        return pl.BlockSpec((tt, width), lambda l, b, t: (jnp.where(l == j, b * nt + t, 0), col))

    out_spec = pl.BlockSpec((1, 1, tt, DA_HEADS, hw), lambda l, b, t: (l, b, t, 0, 0))
    out_shape = jax.ShapeDtypeStruct((n_layers, bsz, t_len, DA_HEADS, hw), F32)
    in_specs, args = [], []
    for j, p in enumerate(projs):
        in_specs += [src(j, k_col), src(j, v_col)]
        args += [p, p]
    return pl.pallas_call(
        functools.partial(_kv_layout_kernel, n_layers=n_layers),
        grid=(n_layers, bsz, nt),
        in_specs=in_specs,
        out_specs=[out_spec, out_spec],
        out_shape=[out_shape, out_shape],
        compiler_params=_cparams(3),
        name="kv_layout",
    )(*args)


def _tail_rows(buf, rows, cols, t_valid, n):
    take = min(n, t_valid)
    new = rows[:, t_valid - take:t_valid, cols]
    if take == n:
        return new
    return jnp.concatenate([buf[:, buf.shape[1] - (n - take):], new], axis=1)


def _pad_front(buf, rows):
    return jnp.pad(buf, ((0, 0), (rows - buf.shape[1], 0), (0, 0)))


class MatmulWeights:
    def __init__(self, depth, sources, ready):
        self.depth = depth
        self._sources = sources
        self.ready = ready

    def get(self, name, layer):
        return self.ready[name][layer]

    def pending(self, name, layer):
        if layer >= self.depth or self.ready[name][layer] is not None:
            return None
        return self._sources[name](layer)

    def run(self, fn, target, *args, **kwargs):
        cast = self.pending(*target) if target is not None else None
        res = fn(*args, cast=cast, **kwargs)
        if cast is None:
            return res
        out, self.ready[target[0]][target[1]] = res
        return out


def _trunk(x, pemb, wts, mmw, states, *, bsz, t_len, t_valid, tm, tn, tt, paged):
    depth = pemb.shape[0]
    d_model = x.shape[1]
    pool_w = d_model // 2
    da_w = DA_HEADS * 2 * DA_HEAD_DIM
    hw = 2 * DA_HEAD_DIM
    conv_w = d_model // 2
    inner = d_model // 2
    gn = SSM_GROUPS * SSM_STATE
    xbc_w = inner + 2 * gn
    tm2 = min(2 * tm, bsz * t_len)
    pos0 = 0 if paged is None else paged[2].shape[1] * paged[0].shape[2]
    h = x
    outs = dict(k=[], v=[], pool=[], conf=[], mconv=[], ssm=[])
    even_projs = []
    q_col = pool_w // hw

    def pad_seq_rows(a, t_new):
        return jnp.pad(a.reshape(bsz, t_len, -1),
                       ((0, 0), (0, t_new - t_len), (0, 0))).reshape(bsz * t_new, -1)

    for i in range(depth):
        j = i // 2
        if i % 2 == 0:
            proj = mmw.run(norm_matmul, ("w_out", i), h, wts["g_mix"][i], mmw.get("w_in", i),
                           tm=tm, tn=tn)
            rows = proj.reshape(bsz, t_len, -1)
            pool_buf = states["pool"][j]
            outs["pool"].append(_tail_rows(pool_buf, rows, slice(0, pool_w), t_valid, POOL_BUF))
            z_pool = pool_mixer(proj, _pad_front(pool_buf, POOL_CARRY), wts["pool_w"], j,
                                wts["pool_scale"][j], bsz=bsz, t_len=t_len, tt=tt, pos0=pos0)
            lam_init = 0.8 - 0.6 * math.exp(-0.3 * i)
            lam_vecs = jnp.stack([wts["lambda_q1"][j], wts["lambda_k1"][j],
                                  wts["lambda_q2"][j], wts["lambda_k2"][j]]).astype(F32)
            if paged is None:
                even_projs.append(proj)
                attn = attention_prompt(
                    proj, lam_vecs, wts["subln_g"][j], bsz=bsz, t_len=t_len, q_col=q_col,
                    k_col=q_col + DA_HEADS, v_col=q_col + 2 * DA_HEADS, tq=512, lam_init=lam_init)
            else:
                cache_k, cache_v, page_table = paged
                page = cache_k.shape[2]
                k_seq = rows[:, :t_valid, pool_w + da_w:pool_w + 2 * da_w].reshape(
                    bsz, t_valid, DA_HEADS, hw)
                v_seq = rows[:, :t_valid, pool_w + 2 * da_w:pool_w + 3 * da_w].reshape(
                    bsz, t_valid, DA_HEADS, hw)
                outs["k"].append(k_seq)
                outs["v"].append(v_seq)
                new_page = lambda r: jnp.pad(r, ((0, 0), (0, page - t_valid), (0, 0), (0, 0)))
                attn = attention_sample(
                    proj, cache_k, cache_v, j, page_table, new_page(k_seq), new_page(v_seq),
                    lam_vecs, wts["subln_g"][j], bsz=bsz, tpad=t_len, q_col=pool_w // da_w,
                    n_pages=8, lam_init=lam_init)
            h = mmw.run(matmul2_residual, ("w_gate", i), z_pool, attn, mmw.get("w_out", i), h,
                        tm=tm2, tn=tn // 2)
        else:
            x0 = 2 * conv_w + inner
            proj = mmw.run(norm_matmul, ("w_out", i), h, wts["g_mix"][i], mmw.get("w_in", i),
                           tm=tm, tn=tn)
            dt_raw = norm_matmul(h, wts["g_mix"][i], wts["w_dt"][j], tm=tm, tn=LANES)
            rows = proj.reshape(bsz, t_len, -1)
            conf_buf = states["conf"][j]
            zc, glu = conformer_mixer(
                proj, _pad_front(conf_buf, CONV_CARRY), wts["conf_dw_w"][j], wts["conf_dw_b"][j],
                wts["conf_ln_g"][j], wts["conf_ln_b"][j], wts["conf_pw_w"], j, wts["conf_pw_b"][j],
                bsz=bsz, t_len=t_len, tt=tt)
            outs["conf"].append(_tail_rows(conf_buf, glu.reshape(bsz, t_len, conv_w),
                                           slice(None), t_valid, CONV_K - 1))
            mconv_buf = states["mconv"][j]
            outs["mconv"].append(_tail_rows(mconv_buf, rows, slice(x0, x0 + xbc_w), t_valid,
                                            SSM_CONV_K - 1))
            t_ssd = -(-t_len // SSM_CHUNK) * SSM_CHUNK
            proj_ssd = proj
            if t_ssd != t_len:
                proj_ssd = pad_seq_rows(proj, t_ssd)
                dt_raw = pad_seq_rows(dt_raw, t_ssd)
            y, h_new = ssd_mixer(
                proj_ssd, dt_raw, _pad_front(mconv_buf, SSM_CARRY), states["ssm"][j],
                wts["ssm_conv_w"][j], wts["ssm_conv_b"][j], wts["ssm_dt_bias"][j],
                wts["ssm_A_log"][j], wts["ssm_D"][j], wts["ssm_norm_g"][j],
                bsz=bsz, t_len=t_ssd, t_valid=min(t_valid, SSM_CHUNK),
                z_col=2 * conv_w // inner, x_col=x0 // inner, b_col=(x0 + inner) // gn,
                c_col=(x0 + inner + gn) // gn)
            if t_ssd != t_len:
                y = y.reshape(bsz, t_ssd, inner)[:, :t_len].reshape(bsz * t_len, inner)
            outs["ssm"].append(h_new)
            h = mmw.run(matmul2_residual, ("w_gate", i), zc, y, mmw.get("w_out", i), h,
                        tm=tm2, tn=tn // 2)
        hidden = mmw.run(norm_matmul, ("w_down", i), h, wts["g_mlp"][i], mmw.get("w_up", i),
                         tm=tm, tn=tn, act="relu2", out_dtype=BF16)
        h = mmw.run(matmul_k_residual, ("w_up", i + 1), hidden, mmw.get("w_down", i), h,
                    tm=tm2, tn=tn, tk=2 * tn)
        h = mmw.run(ple_update, ("w_in", i + 1), h, wts["g_ple"][i], mmw.get("w_gate", i), pemb,
                    wts["w_ple_proj"], i, tm=tm, tn=tn // 2)
    if paged is None:
        outs["k"], outs["v"] = kv_layout(even_projs, bsz=bsz, t_len=t_len, tt=tt,
                                         k_col=pool_w // da_w + 1, v_col=pool_w // da_w + 2)
    else:
        outs["k"], outs["v"] = jnp.stack(outs["k"]), jnp.stack(outs["v"])
    y = rmsnorm(h, wts["g_final"], tm=min(256, bsz * t_len))
    return y, outs


def kernel(x_prompt, x_sample, p_prompt, p_sample, cache_k, cache_v, page_table, state_pool, state_conf_conv, state_ssm_conv, state_ssm, g_mix, g_mlp, g_ple, g_final, w_in_even, pool_w, pool_scale, lambda_q1, lambda_k1, lambda_q2, lambda_k2, subln_g, w_out_even, w_in_odd, conf_dw_w, conf_dw_b, conf_ln_g, conf_ln_b, conf_pw_w, conf_pw_b, ssm_conv_w, ssm_conv_b, ssm_dt_bias, ssm_A_log, ssm_D, ssm_norm_g, w_out_odd, w_up, w_down, w_ple_proj, w_ple_gate):
    bp, tp, d_model = x_prompt.shape
    bs, ts, _ = x_sample.shape
    depth = p_prompt.shape[0]
    n_even, n_odd = (depth + 1) // 2, depth // 2
    inner = ssm_norm_g.shape[-1]
    n_main = w_in_odd.shape[-1] - inner // SSM_HEAD_DIM
    w_dt = jnp.pad(w_in_odd[:, :, n_main:].astype(BF16),
                   ((0, 0), (0, 0), (0, LANES - (w_in_odd.shape[-1] - n_main))))
    wts = dict(
        w_dt=w_dt,
        g_mix=g_mix, g_mlp=g_mlp, g_ple=g_ple, g_final=g_final,
        pool_w=pool_w.astype(BF16), pool_scale=pool_scale,
        lambda_q1=lambda_q1, lambda_k1=lambda_k1, lambda_q2=lambda_q2, lambda_k2=lambda_k2,
        subln_g=subln_g,
        conf_dw_w=conf_dw_w, conf_dw_b=conf_dw_b, conf_ln_g=conf_ln_g, conf_ln_b=conf_ln_b,
        conf_pw_w=conf_pw_w.astype(BF16), conf_pw_b=conf_pw_b, ssm_conv_w=ssm_conv_w,
        ssm_conv_b=ssm_conv_b, ssm_dt_bias=ssm_dt_bias, ssm_A_log=ssm_A_log, ssm_D=ssm_D,
        ssm_norm_g=ssm_norm_g, w_ple_proj=w_ple_proj.astype(BF16))
    sources = dict(
        w_in=lambda i: (WeightCast(w_in_even, i // 2, w_in_even.shape[-1]) if i % 2 == 0
                        else WeightCast(w_in_odd, i // 2, n_main)),
        w_out=lambda i: WeightCast(w_out_even if i % 2 == 0 else w_out_odd, i // 2, d_model),
        w_up=lambda i: WeightCast(w_up, i, w_up.shape[-1]),
        w_down=lambda i: WeightCast(w_down, i, d_model),
        w_gate=lambda i: WeightCast(w_ple_gate, i, d_model))
    ready = {name: [None] * depth for name in sources}
    ready["w_in"][0] = w_in_even[0].astype(BF16)
    ready["w_up"][0] = w_up[0].astype(BF16)
    mmw = MatmulWeights(depth, sources, ready)
    dtype = x_prompt.dtype

    zero_states = dict(
        pool=jnp.zeros((n_even, bp, POOL_BUF, pool_scale.shape[-1]), dtype),
        conf=jnp.zeros((n_odd, bp, CONV_K - 1, conf_dw_b.shape[-1]), dtype),
        mconv=jnp.zeros((n_odd, bp, SSM_CONV_K - 1, ssm_conv_b.shape[-1]), dtype),
        ssm=jnp.zeros((n_odd, bp, inner, SSM_STATE), F32))
    y_p, o_p = _trunk(
        x_prompt.reshape(bp * tp, d_model), p_prompt.reshape(depth, bp * tp, -1).astype(BF16), wts,
        mmw, zero_states, bsz=bp, t_len=tp, t_valid=tp, tm=512, tn=1024, tt=256, paged=None)

    t_pad = SAMPLE_T_PAD
    pad_t = lambda a, axis: jnp.pad(a, [(0, t_pad - ts) if d == axis else (0, 0) for d in range(a.ndim)])
    sample_states = dict(pool=state_pool, conf=state_conf_conv, mconv=state_ssm_conv,
                         ssm=state_ssm.astype(F32).reshape(n_odd, bs, inner, SSM_STATE))
    y_s, o_s = _trunk(
        pad_t(x_sample, 1).reshape(bs * t_pad, d_model),
        pad_t(p_sample, 2).reshape(depth, bs * t_pad, -1).astype(BF16), wts, mmw, sample_states,
        bsz=bs, t_len=t_pad, t_valid=ts, tm=bs * t_pad, tn=2048, tt=t_pad,
        paged=(cache_k, cache_v, page_table))

    def ssm_out(states, bsz):
        return jnp.stack(states).reshape(n_odd, bsz, inner // SSM_HEAD_DIM, SSM_HEAD_DIM,
                                         SSM_STATE).astype(state_ssm.dtype)

    return (y_p.reshape(bp, tp, d_model), y_s.reshape(bs, t_pad, d_model)[:, :ts],
            o_p["k"], o_p["v"], o_s["k"], o_s["v"],
            jnp.stack(o_p["pool"]), jnp.stack(o_s["pool"]),
            jnp.stack(o_p["conf"]), jnp.stack(o_s["conf"]),
            jnp.stack(o_p["mconv"]), jnp.stack(o_s["mconv"]),
            ssm_out(o_p["ssm"], bp), ssm_out(o_s["ssm"], bs))
```

```python
import functools
import math
from typing import NamedTuple

import jax
import jax.numpy as jnp
from jax import lax
from jax.experimental import pallas as pl
from jax.experimental.pallas import tpu as pltpu

F32 = jnp.float32
BF16 = jnp.bfloat16
EPS = 1e-6
NEG_INF = -1e30

V7X_VMEM_BYTES = 64 * 1024 * 1024
VMEM_LIMIT = V7X_VMEM_BYTES - 4 * 1024 * 1024

LANES = 128
SUBLANES = 8
BF16_SUBLANES = 16
POOL_WINDOWS = (2, 4, 8, 16)
POOL_BUF = max(POOL_WINDOWS) - 1
POOL_CARRY = 16
CONV_K = 31
CONV_CARRY = 32
SSM_CONV_K = 4
SSM_CARRY = 8
SSM_CHUNK = 128
DA_HEADS = 8
DA_HEAD_DIM = 128
SSM_HEAD_DIM = 64
SSM_GROUPS = 8
SSM_STATE = 128
SAMPLE_T_PAD = 16


def _cparams(n_axes):
    return pltpu.CompilerParams(
        dimension_semantics=("arbitrary",) * n_axes, vmem_limit_bytes=VMEM_LIMIT)


def _rms_rows(x, g):
    ms = jnp.mean(x * x, axis=-1, keepdims=True)
    return x * lax.rsqrt(ms + EPS) * g


def _sigmoid(x):
    return 0.5 * jnp.tanh(0.5 * x) + 0.5


def _normalise_row_tile(x_hbm, g_ref, xbuf_ref, sem_ref, xn_ref):
    i = pl.program_id(0)
    tm = xn_ref.shape[0]

    def row_copy(tile, slot):
        return pltpu.make_async_copy(x_hbm.at[pl.ds(tile * tm, tm), :], xbuf_ref.at[slot],
                                     sem_ref.at[slot])

    @pl.when(pl.program_id(1) == 0)
    def _():
        slot = lax.rem(i, 2)

        @pl.when(i == 0)
        def _():
            row_copy(0, 0).start()

        row_copy(i, slot).wait()

        @pl.when(i + 1 < pl.num_programs(0))
        def _():
            row_copy(i + 1, 1 - slot).start()

        xn_ref[...] = _rms_rows(xbuf_ref[slot], g_ref[...]).astype(BF16)


def _row_tile_scratch(tm, k):
    return [pltpu.VMEM((2, tm, k), F32), pltpu.SemaphoreType.DMA((2,)), pltpu.VMEM((tm, k), BF16)]


class WeightCast(NamedTuple):
    src: jax.Array
    layer: int
    n_cols: int
    transposed: bool = False


def _cast_specs(cast, grid):
    n_steps = math.prod(grid)

    def tile_fn(n_tiles):
        assert n_tiles <= n_steps

        def tile(*g):
            step = g[0]
            for idx, extent in zip(g[1:], grid[1:]):
                step = step * extent + idx
            return jnp.minimum(step, n_tiles - 1)
        return tile

    if cast.transposed:
        rows = cast.src.shape[2]
        c_blk = LANES * max(1, 1 << math.ceil(math.log2(cast.n_cols / LANES / n_steps)))
        assert cast.n_cols % c_blk == 0
        tile = tile_fn(cast.n_cols // c_blk)
        return (pl.BlockSpec((None, c_blk, rows), lambda *g: (cast.layer, tile(*g), 0)),
                pl.BlockSpec((rows, c_blk), lambda *g: (0, tile(*g))),
                jax.ShapeDtypeStruct((rows, cast.n_cols), BF16))
    rows = cast.src.shape[1]
    r_blk = max(BF16_SUBLANES, 1 << math.ceil(math.log2(rows / n_steps)))
    assert rows % r_blk == 0
    tile = tile_fn(rows // r_blk)
    return (pl.BlockSpec((None, r_blk, cast.n_cols), lambda *g: (cast.layer, tile(*g), 0)),
            pl.BlockSpec((r_blk, cast.n_cols), lambda *g: (tile(*g), 0)),
            jax.ShapeDtypeStruct((rows, cast.n_cols), BF16))


def _dense_call(kernel_fn, grid, in_specs, args, out_spec, out_shape, scratch, cast, name):
    out_specs, out_shapes = [out_spec], [out_shape]
    if cast is not None:
        c_in, c_out, c_shape = _cast_specs(cast, grid)
        in_specs, args = in_specs + [c_in], args + [cast.src]
        out_specs, out_shapes = out_specs + [c_out], out_shapes + [c_shape]
    res = pl.pallas_call(
        functools.partial(kernel_fn, n_in=len(in_specs), has_cast=cast is not None),
        grid=grid, in_specs=in_specs, out_specs=out_specs, out_shape=out_shapes,
        scratch_shapes=scratch, compiler_params=_cparams(len(grid)), name=name)(*args)
    return res if cast is not None else res[0]


def _split_refs(refs, n_in, has_cast):
    n_out = 2 if has_cast else 1
    ins, outs, scratch = refs[:n_in], refs[n_in:n_in + n_out], refs[n_in + n_out:]
    if has_cast:
        src = ins[-1][...]
        if src.shape != outs[1].shape:
            src = src.T
        outs[1][...] = src.astype(BF16)
        ins = ins[:-1]
    return ins, outs[0], scratch


def _norm_mm_kernel(*refs, n_in, has_cast, act):
    (x_hbm, g_ref, w_ref), o_ref, (xbuf_ref, sem_ref, xn_ref) = _split_refs(refs, n_in, has_cast)
    _normalise_row_tile(x_hbm, g_ref, xbuf_ref, sem_ref, xn_ref)
    acc = jnp.dot(xn_ref[...], w_ref[...], preferred_element_type=F32)
    if act == "relu2":
        acc = jnp.square(jnp.maximum(acc, 0.0))
    o_ref[...] = acc.astype(o_ref.dtype)


def norm_matmul(x, g, w, *, tm, tn, act=None, out_dtype=F32, cast=None):
    m, k = x.shape
    n = w.shape[1]
    assert m % tm == 0 and n % tn == 0
    return _dense_call(
        functools.partial(_norm_mm_kernel, act=act), (m // tm, n // tn),
        [pl.BlockSpec(memory_space=pl.ANY),
         pl.BlockSpec((1, k), lambda i, j: (0, 0)),
         pl.BlockSpec((k, tn), lambda i, j: (0, j))],
        [x, g.reshape(1, k), w],
        pl.BlockSpec((tm, tn), lambda i, j: (i, j)), jax.ShapeDtypeStruct((m, n), out_dtype),
        _row_tile_scratch(tm, k), cast, "norm_matmul")


def _mm2_res_kernel(*refs, n_in, has_cast):
    (a1_ref, a2_ref, w_ref, r_ref), o_ref, _ = _split_refs(refs, n_in, has_cast)
    k1 = a1_ref.shape[1]
    acc = jnp.dot(a1_ref[...], w_ref[:k1, :], preferred_element_type=F32)
    acc += jnp.dot(a2_ref[...], w_ref[k1:, :], preferred_element_type=F32)
    o_ref[...] = r_ref[...] + acc


def matmul2_residual(a1, a2, w, res, *, tm, tn, cast=None):
    m, k1 = a1.shape
    k2 = a2.shape[1]
    n = w.shape[1]
    assert m % tm == 0 and n % tn == 0 and w.shape[0] == k1 + k2
    return _dense_call(
        _mm2_res_kernel, (m // tm, n // tn),
        [pl.BlockSpec((tm, k1), lambda i, j: (i, 0)),
         pl.BlockSpec((tm, k2), lambda i, j: (i, 0)),
         pl.BlockSpec((k1 + k2, tn), lambda i, j: (0, j)),
         pl.BlockSpec((tm, tn), lambda i, j: (i, j))],
        [a1, a2, w, res],
        pl.BlockSpec((tm, tn), lambda i, j: (i, j)), jax.ShapeDtypeStruct((m, n), F32),
        [], cast, "matmul2_residual")


def _mmk_res_kernel(*refs, n_in, has_cast):
    (a_ref, w_ref, r_ref), o_ref, _ = _split_refs(refs, n_in, has_cast)

    @pl.when(pl.program_id(2) == 0)
    def _():
        o_ref[...] = r_ref[...]

    o_ref[...] += jnp.dot(a_ref[...], w_ref[...], preferred_element_type=F32)


def matmul_k_residual(a, w, res, *, tm, tn, tk, cast=None):
    m, k = a.shape
    n = w.shape[1]
    assert m % tm == 0 and n % tn == 0 and k % tk == 0
    return _dense_call(
        _mmk_res_kernel, (m // tm, n // tn, k // tk),
        [pl.BlockSpec((tm, tk), lambda i, j, l: (i, l)),
         pl.BlockSpec((tk, tn), lambda i, j, l: (l, j)),
         pl.BlockSpec((tm, tn), lambda i, j, l: (i, j))],
        [a, w, res],
        pl.BlockSpec((tm, tn), lambda i, j, l: (i, j)), jax.ShapeDtypeStruct((m, n), F32),
        [], cast, "matmul_k_residual")


def _ple_kernel(*refs, n_in, has_cast):
    ((x_hbm, g_ref, wg_ref, pe_ref, wp_ref, r_ref), o_ref,
     (xbuf_ref, sem_ref, xn_ref)) = _split_refs(refs, n_in, has_cast)
    _normalise_row_tile(x_hbm, g_ref, xbuf_ref, sem_ref, xn_ref)
    gate = jax.nn.sigmoid(jnp.dot(xn_ref[...], wg_ref[...], preferred_element_type=F32))
    proj = jnp.dot(pe_ref[...], wp_ref[...], preferred_element_type=F32)
    o_ref[...] = r_ref[...] + gate * proj


def ple_update(h, g, wg, pe, wp, layer, *, tm, tn, cast=None):
    m, k = h.shape
    n = wg.shape[1]
    kp = pe.shape[2]
    assert m % tm == 0 and n % tn == 0
    return _dense_call(
        _ple_kernel, (m // tm, n // tn),
        [pl.BlockSpec(memory_space=pl.ANY),
         pl.BlockSpec((1, k), lambda i, j: (0, 0)),
         pl.BlockSpec((k, tn), lambda i, j: (0, j)),
         pl.BlockSpec((None, tm, kp), lambda i, j: (layer, i, 0)),
         pl.BlockSpec((None, kp, tn), lambda i, j: (layer, 0, j)),
         pl.BlockSpec((tm, tn), lambda i, j: (i, j))],
        [h, g.reshape(1, k), wg, pe, wp, h],
        pl.BlockSpec((tm, tn), lambda i, j: (i, j)), jax.ShapeDtypeStruct((m, n), F32),
        _row_tile_scratch(tm, k), cast, "ple_update")


def _rmsnorm_kernel(x_ref, g_ref, o_ref):
    o_ref[...] = _rms_rows(x_ref[...], g_ref[...])


def rmsnorm(x, g, *, tm):
    m, k = x.shape
    assert m % tm == 0
    return pl.pallas_call(
        _rmsnorm_kernel,
        grid=(m // tm,),
        in_specs=[pl.BlockSpec((tm, k), lambda i: (i, 0)), pl.BlockSpec((1, k), lambda i: (0, 0))],
        out_specs=pl.BlockSpec((tm, k), lambda i: (i, 0)),
        out_shape=jax.ShapeDtypeStruct((m, k), F32),
        compiler_params=_cparams(1),
        name="final_rmsnorm",
    )(x, g.reshape(1, k))


def _pool_kernel(u_ref, buf_ref, w_ref, scale_ref, z_ref, full_ref, *, tt, pos0):
    t = pl.program_id(1)

    @pl.when(t == 0)
    def _():
        full_ref[0:POOL_CARRY, :] = buf_ref[0]

    @pl.when(t != 0)
    def _():
        full_ref[0:POOL_CARRY, :] = full_ref[tt:tt + POOL_CARRY, :]

    full_ref[POOL_CARRY:POOL_CARRY + tt, :] = u_ref[...]
    gc = w_ref.shape[1]
    pos = pos0 + t * tt + lax.broadcasted_iota(jnp.int32, (tt, 1), 0)
    for g, win in enumerate(POOL_WINDOWS):
        cols = slice(g * gc, (g + 1) * gc)
        cur = full_ref[POOL_CARRY:POOL_CARRY + tt, cols]
        tot = cur
        for i in range(1, win):
            tot = tot + full_ref[POOL_CARRY - i:POOL_CARRY - i + tt, cols]
        cnt = jnp.minimum(pos + 1, win).astype(F32)
        d = tot / cnt - cur
        z = jnp.dot(d.astype(BF16), w_ref[g], preferred_element_type=F32)
        z_ref[:, cols] = (z * scale_ref[:, cols]).astype(z_ref.dtype)


def pool_mixer(proj, buf16, w, layer, scale, *, bsz, t_len, tt, pos0):
    c = scale.shape[-1]
    nt = t_len // tt
    assert t_len % tt == 0 and (nt == 1 or tt >= POOL_CARRY)
    return pl.pallas_call(
        functools.partial(_pool_kernel, tt=tt, pos0=pos0),
        grid=(bsz, nt),
        in_specs=[
            pl.BlockSpec((tt, c), lambda b, t: (b * nt + t, 0)),
            pl.BlockSpec((1, POOL_CARRY, c), lambda b, t: (b, 0, 0)),
            pl.BlockSpec((None,) + w.shape[1:], lambda b, t: (layer, 0, 0, 0)),
            pl.BlockSpec((1, c), lambda b, t: (0, 0)),
        ],
        out_specs=pl.BlockSpec((tt, c), lambda b, t: (b * nt + t, 0)),
        out_shape=jax.ShapeDtypeStruct((bsz * t_len, c), BF16),
        scratch_shapes=[pltpu.VMEM((POOL_CARRY + tt, c), F32)],
        compiler_params=_cparams(2),
        name="pool_mixer",
    )(proj, buf16, w, scale.reshape(1, c))


def _lambda_value(lam_ref, lam_init):
    v = lam_ref[...]
    s1 = jnp.sum(v[0:1] * v[1:2], axis=-1, keepdims=True)
    s2 = jnp.sum(v[2:3] * v[3:4], axis=-1, keepdims=True)
    return jnp.exp(s1) - jnp.exp(s2) + lam_init


def _head_slopes(head_idx_f32):
    return jnp.exp2((-8.0 / DA_HEADS) * (head_idx_f32 + 1.0))


def _subln(o, g, lam_init):
    ms = jnp.mean(o * o, axis=-1, keepdims=True)
    return o * lax.rsqrt(ms + EPS) * g * (1.0 - lam_init)


LOG2E = math.log2(math.e)


def _softmax_tile(s, v_bf16, m_ref, l_ref, acc_ref, idx):
    cols = s.shape[1]
    m_prev = m_ref[idx]
    m_next = jnp.maximum(m_prev, jnp.max(s, axis=-1, keepdims=True))
    alpha = jnp.exp2(m_prev - m_next)
    p = jnp.exp2(s - jnp.tile(m_next, (1, cols // LANES)))
    part = p[:, 0:LANES]
    for i in range(1, cols // LANES):
        part = part + p[:, i * LANES:(i + 1) * LANES]
    l_ref[idx] = alpha * l_ref[idx] + part
    acc_ref[idx] = jnp.tile(alpha, (1, acc_ref.shape[-1] // LANES)) * acc_ref[idx] + jnp.dot(
        p.astype(BF16), v_bf16, preferred_element_type=F32)
    m_ref[idx] = m_next


def _attn_prompt_kernel(q_ref, k_ref, v_ref, lam_ref, g_ref, o_ref, kb_ref, vb_ref, m_ref, l_ref,
                        acc_ref, *, tq, lam_init):
    hd = DA_HEAD_DIM
    h = pl.program_id(1)
    qi = pl.program_id(2)

    @pl.when(qi == 0)
    def _():
        kb_ref[...] = k_ref[...].astype(BF16)
        vb_ref[...] = v_ref[...].astype(BF16)

    slope2 = _head_slopes(jnp.full((1, 1), h, jnp.int32).astype(F32)) * LOG2E
    q = (q_ref[...] * (hd ** -0.5 * LOG2E)).astype(BF16)
    m_ref[...] = jnp.full(m_ref.shape, NEG_INF, F32)
    l_ref[...] = jnp.zeros(l_ref.shape, F32)
    acc_ref[...] = jnp.zeros(acc_ref.shape, F32)
    k_local = lax.broadcasted_iota(jnp.int32, (1, tq), 1)

    def tile(ki, diagonal):
        start = pl.multiple_of(ki * tq, tq)
        k = kb_ref[pl.ds(start, tq), :]
        v = vb_ref[pl.ds(start, tq), :]
        col_bias = slope2 * (k_local + ki * tq).astype(F32)
        for c in range(2):
            s = lax.dot_general(q[:, c * hd:(c + 1) * hd], k[:, c * hd:(c + 1) * hd],
                                (((1,), (1,)), ((), ())), preferred_element_type=F32) + col_bias
            if diagonal:
                keep = (lax.broadcasted_iota(jnp.int32, (tq, tq), 0)
                        >= lax.broadcasted_iota(jnp.int32, (tq, tq), 1))
                s = jnp.where(keep, s, NEG_INF)
            _softmax_tile(s, v, m_ref, l_ref, acc_ref, c)

    def body(ki, carry):
        tile(ki, False)
        return carry

    lax.fori_loop(0, qi, body, 0)
    tile(qi, True)
    lam = _lambda_value(lam_ref, lam_init)
    o = (acc_ref[0] / jnp.sum(l_ref[0], axis=-1, keepdims=True)
         - lam * (acc_ref[1] / jnp.sum(l_ref[1], axis=-1, keepdims=True)))
    o_ref[...] = _subln(o, g_ref[...], lam_init).astype(o_ref.dtype)


def attention_prompt(proj, lam_vecs, subln_g, *, bsz, t_len, q_col, k_col, v_col, tq, lam_init):
    hw = 2 * DA_HEAD_DIM
    nq = t_len // tq
    assert t_len % tq == 0
    return pl.pallas_call(
        functools.partial(_attn_prompt_kernel, tq=tq, lam_init=lam_init),
        grid=(bsz, DA_HEADS, nq),
        in_specs=[
            pl.BlockSpec((tq, hw), lambda b, h, i: (b * nq + i, q_col + h)),
            pl.BlockSpec((t_len, hw), lambda b, h, i: (b, k_col + h)),
            pl.BlockSpec((t_len, hw), lambda b, h, i: (b, v_col + h)),
            pl.BlockSpec((4, DA_HEAD_DIM), lambda b, h, i: (0, 0)),
            pl.BlockSpec((1, hw), lambda b, h, i: (0, 0)),
        ],
        out_specs=pl.BlockSpec((tq, hw), lambda b, h, i: (b * nq + i, h)),
        out_shape=jax.ShapeDtypeStruct((bsz * t_len, DA_HEADS * hw), BF16),
        scratch_shapes=[pltpu.VMEM((t_len, hw), BF16), pltpu.VMEM((t_len, hw), BF16),
                        pltpu.VMEM((2, tq, LANES), F32), pltpu.VMEM((2, tq, LANES), F32),
                        pltpu.VMEM((2, tq, hw), F32)],
        compiler_params=_cparams(3),
        name="attention_prompt",
    )(proj, proj, proj, lam_vecs, subln_g.reshape(1, hw))


def _attn_sample_kernel(pt_ref, q_ref, *refs, n_pages, tpad, pos0, lam_init):
    del pt_ref
    k_refs = refs[:n_pages]
    v_refs = refs[n_pages:2 * n_pages]
    kn_ref, vn_ref, lam_ref, g_ref, o_ref, qall_ref, bias_ref, m_ref, l_ref, acc_ref = refs[2 * n_pages:]
    hd = DA_HEAD_DIM
    nh = DA_HEADS
    page = k_refs[0].shape[2]
    nrow = nh * 2 * tpad
    ncol = page * nh
    sidx = pl.program_id(1)
    sh_row = int(math.log2(2 * tpad))
    sh_col = int(math.log2(nh))
    assert 1 << sh_row == 2 * tpad and 1 << sh_col == nh

    def rel_and_match():
        row = lax.broadcasted_iota(jnp.int32, (nrow, ncol), 0)
        col = lax.broadcasted_iota(jnp.int32, (nrow, ncol), 1)
        rel = (jnp.right_shift(col, sh_col) - jnp.bitwise_and(row, tpad - 1)).astype(F32)
        match = jnp.right_shift(row, sh_row) == jnp.bitwise_and(col, nh - 1)
        return rel, match

    def row_slopes(shape):
        row = lax.broadcasted_iota(jnp.int32, shape, 0)
        return _head_slopes(jnp.right_shift(row, sh_row).astype(F32)) * LOG2E

    @pl.when(sidx == 0)
    def _():
        q = q_ref[...] * (hd ** -0.5 * LOG2E)
        lane = lax.broadcasted_iota(jnp.int32, (tpad, 2 * hd), 1)
        for h in range(nh):
            qh = q[:, h * 2 * hd:(h + 1) * 2 * hd]
            qall_ref[h * 2 * tpad:h * 2 * tpad + tpad, :] = jnp.where(lane < hd, qh, 0.0).astype(BF16)
            qall_ref[h * 2 * tpad + tpad:(h + 1) * 2 * tpad, :] = jnp.where(lane >= hd, qh, 0.0).astype(BF16)
        m_ref[...] = jnp.full(m_ref.shape, NEG_INF, F32)
        l_ref[...] = jnp.zeros(l_ref.shape, F32)
        acc_ref[...] = jnp.zeros(acc_ref.shape, F32)
        rel, match = rel_and_match()
        bias_ref[...] = jnp.where(match, row_slopes((nrow, ncol)) * rel, NEG_INF)

    def scores(k_page):
        kk = k_page.reshape(ncol, 2 * hd).astype(BF16)
        return lax.dot_general(qall_ref[...], kk, (((1,), (1,)), ((), ())), preferred_element_type=F32)

    slope_col = row_slopes((nrow, 1))
    for i in range(n_pages):
        base = ((sidx * n_pages + i) * page - pos0).astype(F32)
        s = scores(k_refs[i][0, 0]) + bias_ref[...] + slope_col * base
        vv = v_refs[i][0, 0].reshape(ncol, 2 * hd).astype(BF16)
        _softmax_tile(s, vv, m_ref, l_ref, acc_ref, 0)

    @pl.when(sidx == pl.num_programs(1) - 1)
    def _():
        rel, _ = rel_and_match()
        s = scores(kn_ref[0]) + jnp.where(rel <= 0.0, bias_ref[...], NEG_INF)
        vv = vn_ref[0].reshape(ncol, 2 * hd).astype(BF16)
        _softmax_tile(s, vv, m_ref, l_ref, acc_ref, 0)
        lam = _lambda_value(lam_ref, lam_init)
        a = acc_ref[0] / jnp.sum(l_ref[0], axis=-1, keepdims=True)
        for h in range(nh):
            r0 = h * 2 * tpad
            o = _subln(a[r0:r0 + tpad] - lam * a[r0 + tpad:r0 + 2 * tpad], g_ref[...], lam_init)
            o_ref[:, h * 2 * hd:(h + 1) * 2 * hd] = o.astype(o_ref.dtype)


def attention_sample(proj, cache_k, cache_v, layer, page_table, k_new, v_new, lam_vecs, subln_g,
                     *, bsz, tpad, q_col, n_pages, lam_init):
    hw = 2 * DA_HEAD_DIM
    qw = DA_HEADS * hw
    page = cache_k.shape[2]
    n_past_pages = page_table.shape[1]
    assert n_past_pages % n_pages == 0
    pos0 = n_past_pages * page
    nrow = DA_HEADS * 2 * tpad
    page_block = (1, 1, page, DA_HEADS, hw)

    def page_spec(i):
        return pl.BlockSpec(page_block, lambda b, s, pt: (layer, pt[b, s * n_pages + i], 0, 0, 0))

    new_spec = pl.BlockSpec((1, page, DA_HEADS, hw), lambda b, s, pt: (b, 0, 0, 0))
    grid_spec = pltpu.PrefetchScalarGridSpec(
        num_scalar_prefetch=1,
        grid=(bsz, n_past_pages // n_pages),
        in_specs=[pl.BlockSpec((tpad, qw), lambda b, s, pt: (b, q_col))]
        + [page_spec(i) for i in range(n_pages)] * 2
        + [new_spec, new_spec,
           pl.BlockSpec((4, DA_HEAD_DIM), lambda b, s, pt: (0, 0)),
           pl.BlockSpec((1, hw), lambda b, s, pt: (0, 0))],
        out_specs=pl.BlockSpec((tpad, qw), lambda b, s, pt: (b, 0)),
        scratch_shapes=[pltpu.VMEM((nrow, hw), BF16), pltpu.VMEM((nrow, page * DA_HEADS), F32),
                        pltpu.VMEM((1, nrow, LANES), F32), pltpu.VMEM((1, nrow, LANES), F32),
                        pltpu.VMEM((1, nrow, hw), F32)],
    )
    return pl.pallas_call(
        functools.partial(_attn_sample_kernel, n_pages=n_pages, tpad=tpad, pos0=pos0, lam_init=lam_init),
        grid_spec=grid_spec,
        out_shape=jax.ShapeDtypeStruct((bsz * tpad, qw), BF16),
        compiler_params=_cparams(2),
        name="attention_sample",
    )(page_table, proj, *([cache_k] * n_pages), *([cache_v] * n_pages), k_new, v_new,
      lam_vecs, subln_g.reshape(1, hw))


def _conformer_kernel(a_ref, gate_ref, buf_ref, dww_ref, dwb_ref, lng_ref, lnb_ref, pww_ref, pwb_ref,
                      z_ref, glu_ref, full_ref, cv_ref, *, tt):
    t = pl.program_id(1)

    @pl.when(t == 0)
    def _():
        full_ref[0:CONV_CARRY, :] = buf_ref[0]

    @pl.when(t != 0)
    def _():
        full_ref[0:CONV_CARRY, :] = full_ref[tt:tt + CONV_CARRY, :]

    glu = a_ref[...] * _sigmoid(gate_ref[...])
    full_ref[CONV_CARRY:CONV_CARRY + tt, :] = glu
    glu_ref[...] = glu
    first = CONV_CARRY - (CONV_K - 1)
    rb = min(tt, 128)
    phases = {}
    for k in range(CONV_K):
        phases.setdefault((first + k) % SUBLANES, []).append(k)

    def lane_chunk(c, carry):
        col = pl.ds(pl.multiple_of(c * LANES, LANES), LANES)
        for r0 in range(0, tt, rb):
            acc = jnp.zeros((rb, LANES), F32) + dwb_ref[:, col]
            for phase, taps in phases.items():
                rows = rb if phase == 0 else rb + SUBLANES
                part = None
                for k in taps:
                    base = r0 + first + k - phase
                    term = dww_ref[k:k + 1, col] * full_ref[base:base + rows, col]
                    part = term if part is None else part + term
                acc = acc + part[phase:phase + rb]
            cv_ref[r0:r0 + rb, col] = acc
        return carry

    lax.fori_loop(0, cv_ref.shape[1] // LANES, lane_chunk, 0)
    y = cv_ref[...]
    mu = jnp.mean(y, axis=-1, keepdims=True)
    yc = y - mu
    var = jnp.mean(yc * yc, axis=-1, keepdims=True)
    y = yc * lax.rsqrt(var + EPS) * lng_ref[...] + lnb_ref[...]
    y = y * _sigmoid(y)
    z = jnp.dot(y.astype(BF16), pww_ref[...], preferred_element_type=F32) + pwb_ref[...]
    z_ref[...] = z.astype(z_ref.dtype)


def conformer_mixer(proj, buf32, dw_w, dw_b, ln_g, ln_b, pw_w, layer, pw_b, *, bsz, t_len, tt):
    c = dw_b.shape[-1]
    nt = t_len // tt
    assert t_len % tt == 0 and (nt == 1 or tt >= CONV_CARRY)
    vec = lambda: pl.BlockSpec((1, c), lambda b, t: (0, 0))
    rows = lambda j: pl.BlockSpec((tt, c), lambda b, t: (b * nt + t, j))
    return pl.pallas_call(
        functools.partial(_conformer_kernel, tt=tt),
        grid=(bsz, nt),
        in_specs=[
            rows(0), rows(1),
            pl.BlockSpec((1, CONV_CARRY, c), lambda b, t: (b, 0, 0)),
            pl.BlockSpec((CONV_K, c), lambda b, t: (0, 0)),
            vec(), vec(), vec(),
            pl.BlockSpec((None, c, c), lambda b, t: (layer, 0, 0)),
            vec(),
        ],
        out_specs=[rows(0), rows(0)],
        out_shape=[jax.ShapeDtypeStruct((bsz * t_len, c), BF16),
                   jax.ShapeDtypeStruct((bsz * t_len, c), F32)],
        scratch_shapes=[pltpu.VMEM((CONV_CARRY + tt, c), F32), pltpu.VMEM((tt, c), F32)],
        compiler_params=_cparams(2),
        name="conformer_mixer",
    )(proj, proj, buf32, dw_w, dw_b.reshape(1, c), ln_g.reshape(1, c), ln_b.reshape(1, c),
      pw_w, pw_b.reshape(1, c))


def _split3(x, axis):
    hi = x.astype(BF16)
    rest = x - hi.astype(F32)
    mid = rest.astype(BF16)
    lo = (rest - mid.astype(F32)).astype(BF16)
    return jnp.concatenate([hi, mid, lo], axis=axis)


def _select_dot_left(sel_bf16, x):
    n = x.shape[1]
    r = jnp.dot(sel_bf16, _split3(x, 1), preferred_element_type=F32)
    return r[:, 0:n] + r[:, n:2 * n] + r[:, 2 * n:3 * n]


def _ssd_kernel(z_ref, x_ref, b_ref, c_ref, dt_ref, buf_ref, cw_ref, cb_ref, dtb_ref, alog_ref,
                dexp_ref, ng_ref, e_ref, et_ref, h0_ref, y_ref, hn_ref, full_ref, st_ref,
                *, t_valid):
    t = pl.program_id(1)
    ln = x_ref.shape[0]
    inner = x_ref.shape[1]
    gn = b_ref.shape[1]
    ns = SSM_STATE
    gw = inner // SSM_GROUPS
    hpg = gw // SSM_HEAD_DIM

    @pl.when(t == 0)
    def _():
        full_ref[0:SSM_CARRY, :] = buf_ref[0]
        st_ref[...] = h0_ref[0]

    @pl.when(t != 0)
    def _():
        full_ref[0:SSM_CARRY, :] = full_ref[ln:ln + SSM_CARRY, :]

    full_ref[SSM_CARRY:SSM_CARRY + ln, 0:inner] = x_ref[...]
    full_ref[SSM_CARRY:SSM_CARRY + ln, inner:inner + gn] = b_ref[...]
    full_ref[SSM_CARRY:SSM_CARRY + ln, inner + gn:inner + 2 * gn] = c_ref[...]
    first = SSM_CARRY - (SSM_CONV_K - 1)
    xc = cb_ref[...] + cw_ref[0:1, :] * full_ref[first:first + ln, :]
    for k in range(1, SSM_CONV_K):
        xc = xc + cw_ref[k:k + 1, :] * full_ref[first + k:first + k + ln, :]
    xc = xc * _sigmoid(xc)
    xs = xc[:, 0:inner]
    bm = xc[:, inner:inner + gn].astype(BF16)
    cm = xc[:, inner + gn:inner + 2 * gn].astype(BF16)

    dt_in = dt_ref[...] + dtb_ref[...]
    dt = jnp.maximum(dt_in, 0.0) + jnp.log1p(jnp.exp(-jnp.abs(dt_in)))
    row = lax.broadcasted_iota(jnp.int32, (ln, ln), 0)
    col = lax.broadcasted_iota(jnp.int32, (ln, ln), 1)
    if t_valid < ln:
        dt = jnp.where(lax.broadcasted_iota(jnp.int32, dt.shape, 0) < t_valid, dt, 0.0)
    ad = dt * (-jnp.exp(alog_ref[...]))
    causal = row >= col
    a_cs = _select_dot_left(jnp.where(causal, 1.0, 0.0).astype(BF16), ad)
    a_cs_t = a_cs.T
    a_last = a_cs[ln - 1:ln, :]
    per_head = jnp.concatenate([dt, jnp.exp(a_last - a_cs), jnp.exp(a_cs)], axis=0)
    per_chan = jnp.dot(_split3(per_head, 1), e_ref[...], preferred_element_type=F32)
    xd = xs * per_chan[0:ln]
    xdw = (xd * per_chan[ln:2 * ln]).astype(BF16)
    xd = xd.astype(BF16)
    exp_acs = per_chan[2 * ln:3 * ln]
    last_t = jnp.broadcast_to(jnp.exp(a_cs_t[:, ln - 1:ln]), a_cs_t.shape)
    st_decay = _select_dot_left(et_ref[...], last_t)

    lane_head = lax.broadcasted_iota(jnp.int32, (ln, gw), 1) // SSM_HEAD_DIM
    for g in range(SSM_GROUPS):
        cols = slice(g * gw, (g + 1) * gw)
        scols = slice(g * ns, (g + 1) * ns)
        scores = lax.dot_general(cm[:, scols], bm[:, scols], (((1,), (1,)), ((), ())),
                                 preferred_element_type=F32)
        lhs, rhs = [], []
        for r in range(hpg):
            h = g * hpg + r
            seg = a_cs[:, h:h + 1] - a_cs_t[h:h + 1, :]
            lhs.append((scores * jnp.where(causal, jnp.exp(seg), 0.0)).astype(BF16))
            rhs.append(jnp.where(lane_head == r, xd[:, cols], jnp.zeros_like(xd[:, cols])))
        y = jnp.dot(jnp.concatenate(lhs, axis=1), jnp.concatenate(rhs, axis=0),
                    preferred_element_type=F32)
        h_prev = st_ref[cols, :]
        y = y + exp_acs[:, cols] * lax.dot_general(
            cm[:, scols], h_prev.astype(BF16), (((1,), (1,)), ((), ())), preferred_element_type=F32)
        new_states = lax.dot_general(xdw[:, cols], bm[:, scols], (((0,), (0,)), ((), ())),
                                     preferred_element_type=F32)
        st_ref[cols, :] = h_prev * st_decay[cols, :] + new_states
        y = y + dexp_ref[:, cols] * xs[:, cols]
        zg = z_ref[:, cols]
        y = y * (zg * _sigmoid(zg))
        y_ref[:, cols] = (_rms_rows(y, ng_ref[:, cols])).astype(y_ref.dtype)

    @pl.when(t == pl.num_programs(1) - 1)
    def _():
        hn_ref[0] = st_ref[...]


def ssd_mixer(proj, dt_raw, buf8, h0, conv_w, conv_b, dt_bias, a_log, d_skip, norm_g,
              *, bsz, t_len, t_valid, z_col, x_col, b_col, c_col):
    inner = norm_g.shape[-1]
    heads = inner // SSM_HEAD_DIM
    gn = SSM_GROUPS * SSM_STATE
    xbc = inner + 2 * gn
    ln = SSM_CHUNK
    nc = t_len // ln
    assert t_len % ln == 0 and heads <= LANES

    def pad_heads(v):
        return jnp.pad(v.astype(F32), (0, LANES - heads)).reshape(1, LANES)

    expand = (jnp.arange(LANES)[:, None] == (jnp.arange(inner) // SSM_HEAD_DIM)[None, :]).astype(BF16)
    d_exp = jnp.repeat(d_skip.astype(F32), SSM_HEAD_DIM).reshape(1, inner)
    const = lambda shape: pl.BlockSpec(shape, lambda b, t: (0,) * len(shape))
    rows = lambda w, j: pl.BlockSpec((ln, w), lambda b, t: (b * nc + t, j))
    return pl.pallas_call(
        functools.partial(_ssd_kernel, t_valid=t_valid),
        grid=(bsz, nc),
        in_specs=[
            rows(inner, z_col), rows(inner, x_col), rows(gn, b_col), rows(gn, c_col),
            rows(LANES, 0),
            pl.BlockSpec((1, SSM_CARRY, xbc), lambda b, t: (b, 0, 0)),
            const((SSM_CONV_K, xbc)), const((1, xbc)),
            const((1, LANES)), const((1, LANES)),
            const((1, inner)), const((1, inner)),
            const((3 * LANES, inner)), const((inner, LANES)),
            pl.BlockSpec((1, inner, SSM_STATE), lambda b, t: (b, 0, 0)),
        ],
        out_specs=[rows(inner, 0), pl.BlockSpec((1, inner, SSM_STATE), lambda b, t: (b, 0, 0))],
        out_shape=[jax.ShapeDtypeStruct((bsz * t_len, inner), BF16),
                   jax.ShapeDtypeStruct((bsz, inner, SSM_STATE), F32)],
        scratch_shapes=[pltpu.VMEM((SSM_CARRY + ln, xbc), F32), pltpu.VMEM((inner, SSM_STATE), F32)],
        compiler_params=_cparams(2),
        name="ssd_mixer",
    )(proj, proj, proj, proj, dt_raw, buf8, conv_w, conv_b.reshape(1, xbc), pad_heads(dt_bias),
      pad_heads(a_log), d_exp, norm_g.reshape(1, inner), jnp.tile(expand, (3, 1)), expand.T, h0)


def _kv_layout_kernel(*refs, n_layers):
    ins = refs[:2 * n_layers]
    ko_ref, vo_ref = refs[2 * n_layers:]
    hw = ko_ref.shape[-1]
    for j in range(n_layers):
        @pl.when(pl.program_id(0) == j)
        def _(j=j):
            for h in range(DA_HEADS):
                ko_ref[0, 0, :, h, :] = ins[2 * j][:, h * hw:(h + 1) * hw]
                vo_ref[0, 0, :, h, :] = ins[2 * j + 1][:, h * hw:(h + 1) * hw]


def kv_layout(projs, *, bsz, t_len, tt, k_col, v_col):
    n_layers = len(projs)
    hw = 2 * DA_HEAD_DIM
    width = DA_HEADS * hw
    nt = t_len // tt
    assert t_len % tt == 0

    def src(j, col):
        return pl.BlockSpec((tt, width), lambda l, b, t: (jnp.where(l == j, b * nt + t, 0), col))

    out_spec = pl.BlockSpec((1, 1, tt, DA_HEADS, hw), lambda l, b, t: (l, b, t, 0, 0))
    out_shape = jax.ShapeDtypeStruct((n_layers, bsz, t_len, DA_HEADS, hw), F32)
    in_specs, args = [], []
    for j, p in enumerate(projs):
        in_specs += [src(j, k_col), src(j, v_col)]
        args += [p, p]
    return pl.pallas_call(
        functools.partial(_kv_layout_kernel, n_layers=n_layers),
        grid=(n_layers, bsz, nt),
        in_specs=in_specs,
        out_specs=[out_spec, out_spec],
        out_shape=[out_shape, out_shape],
        compiler_params=_cparams(3),
        name="kv_layout",
    )(*args)


def _tail_rows(buf, rows, cols, t_valid, n):
    take = min(n, t_valid)
    new = rows[:, t_valid - take:t_valid, cols]
    if take == n:
        return new
    return jnp.concatenate([buf[:, buf.shape[1] - (n - take):], new], axis=1)


def _pad_front(buf, rows):
    return jnp.pad(buf, ((0, 0), (rows - buf.shape[1], 0), (0, 0)))


def _dt_weight_kernel(src_ref, o_ref):
    rows = src_ref[...]
    pad = jnp.zeros((LANES - rows.shape[0], rows.shape[1]), F32)
    o_ref[...] = jnp.concatenate([rows, pad], axis=0).T.astype(BF16)


def dt_weight(w_t, row0, n_rows):
    n_layers, _, k = w_t.shape
    assert row0 % n_rows == 0 and n_rows % SUBLANES == 0 and n_rows <= LANES
    return pl.pallas_call(
        _dt_weight_kernel,
        grid=(n_layers,),
        in_specs=[pl.BlockSpec((None, n_rows, k), lambda l: (l, row0 // n_rows, 0))],
        out_specs=pl.BlockSpec((None, k, LANES), lambda l: (l, 0, 0)),
        out_shape=jax.ShapeDtypeStruct((n_layers, k, LANES), BF16),
        compiler_params=_cparams(1),
        name="dt_weight",
    )(w_t)


class MatmulWeights:
    def __init__(self, depth, sources, ready):
        self.depth = depth
        self._sources = sources
        self.ready = ready

    def get(self, name, layer):
        return self.ready[name][layer]

    def pending(self, name, layer):
        if layer >= self.depth or self.ready[name][layer] is not None:
            return None
        return self._sources[name](layer)

    def run(self, fn, target, *args, **kwargs):
        cast = self.pending(*target) if target is not None else None
        res = fn(*args, cast=cast, **kwargs)
        if cast is None:
            return res
        out, self.ready[target[0]][target[1]] = res
        return out


def _trunk(x, pemb, wts, mmw, states, *, bsz, t_len, t_valid, tm, tn, tt, paged):
    depth = pemb.shape[0]
    d_model = x.shape[1]
    pool_w = d_model // 2
    da_w = DA_HEADS * 2 * DA_HEAD_DIM
    hw = 2 * DA_HEAD_DIM
    conv_w = d_model // 2
    inner = d_model // 2
    gn = SSM_GROUPS * SSM_STATE
    xbc_w = inner + 2 * gn
    tm2 = min(2 * tm, bsz * t_len)
    pos0 = 0 if paged is None else paged[2].shape[1] * paged[0].shape[2]
    h = x
    outs = dict(k=[], v=[], pool=[], conf=[], mconv=[], ssm=[])
    even_projs = []
    q_col = pool_w // hw

    def pad_seq_rows(a, t_new):
        return jnp.pad(a.reshape(bsz, t_len, -1),
                       ((0, 0), (0, t_new - t_len), (0, 0))).reshape(bsz * t_new, -1)

    for i in range(depth):
        j = i // 2
        if i % 2 == 0:
            proj = mmw.run(norm_matmul, ("w_out", i), h, wts["g_mix"][i], mmw.get("w_in", i),
                           tm=tm, tn=tn)
            rows = proj.reshape(bsz, t_len, -1)
            pool_buf = states["pool"][j]
            outs["pool"].append(_tail_rows(pool_buf, rows, slice(0, pool_w), t_valid, POOL_BUF))
            z_pool = pool_mixer(proj, _pad_front(pool_buf, POOL_CARRY), wts["pool_w"], j,
                                wts["pool_scale"][j], bsz=bsz, t_len=t_len, tt=tt, pos0=pos0)
            lam_init = 0.8 - 0.6 * math.exp(-0.3 * i)
            lam_vecs = jnp.stack([wts["lambda_q1"][j], wts["lambda_k1"][j],
                                  wts["lambda_q2"][j], wts["lambda_k2"][j]]).astype(F32)
            if paged is None:
                even_projs.append(proj)
                attn = attention_prompt(
                    proj, lam_vecs, wts["subln_g"][j], bsz=bsz, t_len=t_len, q_col=q_col,
                    k_col=q_col + DA_HEADS, v_col=q_col + 2 * DA_HEADS, tq=512, lam_init=lam_init)
            else:
                cache_k, cache_v, page_table = paged
                page = cache_k.shape[2]
                k_seq = rows[:, :t_valid, pool_w + da_w:pool_w + 2 * da_w].reshape(
                    bsz, t_valid, DA_HEADS, hw)
                v_seq = rows[:, :t_valid, pool_w + 2 * da_w:pool_w + 3 * da_w].reshape(
                    bsz, t_valid, DA_HEADS, hw)
                outs["k"].append(k_seq)
                outs["v"].append(v_seq)
                new_page = lambda r: jnp.pad(r, ((0, 0), (0, page - t_valid), (0, 0), (0, 0)))
                attn = attention_sample(
                    proj, cache_k, cache_v, j, page_table, new_page(k_seq), new_page(v_seq),
                    lam_vecs, wts["subln_g"][j], bsz=bsz, tpad=t_len, q_col=pool_w // da_w,
                    n_pages=8, lam_init=lam_init)
            h = mmw.run(matmul2_residual, ("w_gate", i), z_pool, attn, mmw.get("w_out", i), h,
                        tm=tm2, tn=tn // 2)
        else:
            x0 = 2 * conv_w + inner
            proj = mmw.run(norm_matmul, ("w_out", i), h, wts["g_mix"][i], mmw.get("w_in", i),
                           tm=tm, tn=tn)
            dt_raw = norm_matmul(h, wts["g_mix"][i], wts["w_dt"][j], tm=tm, tn=LANES)
            rows = proj.reshape(bsz, t_len, -1)
            conf_buf = states["conf"][j]
            zc, glu = conformer_mixer(
                proj, _pad_front(conf_buf, CONV_CARRY), wts["conf_dw_w"][j], wts["conf_dw_b"][j],
                wts["conf_ln_g"][j], wts["conf_ln_b"][j], wts["conf_pw_w"], j, wts["conf_pw_b"][j],
                bsz=bsz, t_len=t_len, tt=tt)
            outs["conf"].append(_tail_rows(conf_buf, glu.reshape(bsz, t_len, conv_w),
                                           slice(None), t_valid, CONV_K - 1))
            mconv_buf = states["mconv"][j]
            outs["mconv"].append(_tail_rows(mconv_buf, rows, slice(x0, x0 + xbc_w), t_valid,
                                            SSM_CONV_K - 1))
            t_ssd = -(-t_len // SSM_CHUNK) * SSM_CHUNK
            proj_ssd = proj
            if t_ssd != t_len:
                proj_ssd = pad_seq_rows(proj, t_ssd)
                dt_raw = pad_seq_rows(dt_raw, t_ssd)
            y, h_new = ssd_mixer(
                proj_ssd, dt_raw, _pad_front(mconv_buf, SSM_CARRY), states["ssm"][j],
                wts["ssm_conv_w"][j], wts["ssm_conv_b"][j], wts["ssm_dt_bias"][j],
                wts["ssm_A_log"][j], wts["ssm_D"][j], wts["ssm_norm_g"][j],
                bsz=bsz, t_len=t_ssd, t_valid=min(t_valid, SSM_CHUNK),
                z_col=2 * conv_w // inner, x_col=x0 // inner, b_col=(x0 + inner) // gn,
                c_col=(x0 + inner + gn) // gn)
            if t_ssd != t_len:
                y = y.reshape(bsz, t_ssd, inner)[:, :t_len].reshape(bsz * t_len, inner)
            outs["ssm"].append(h_new)
            h = mmw.run(matmul2_residual, ("w_gate", i), zc, y, mmw.get("w_out", i), h,
                        tm=tm2, tn=tn // 2)
        hidden = mmw.run(norm_matmul, ("w_down", i), h, wts["g_mlp"][i], mmw.get("w_up", i),
                         tm=tm, tn=tn, act="relu2", out_dtype=BF16)
        h = mmw.run(matmul_k_residual, ("w_up", i + 1), hidden, mmw.get("w_down", i), h,
                    tm=tm2, tn=tn, tk=2 * tn)
        h = mmw.run(ple_update, ("w_in", i + 1), h, wts["g_ple"][i], mmw.get("w_gate", i), pemb,
                    wts["w_ple_proj"], i, tm=tm, tn=tn // 2)
    if paged is None:
        outs["k"], outs["v"] = kv_layout(even_projs, bsz=bsz, t_len=t_len, tt=tt,
                                         k_col=pool_w // da_w + 1, v_col=pool_w // da_w + 2)
    else:
        outs["k"], outs["v"] = jnp.stack(outs["k"]), jnp.stack(outs["v"])
    y = rmsnorm(h, wts["g_final"], tm=min(256, bsz * t_len))
    return y, outs


def kernel(x_prompt, x_sample, p_prompt, p_sample, cache_k, cache_v, page_table, state_pool, state_conf_conv, state_ssm_conv, state_ssm, g_mix, g_mlp, g_ple, g_final, w_in_even, pool_w, pool_scale, lambda_q1, lambda_k1, lambda_q2, lambda_k2, subln_g, w_out_even, w_in_odd, conf_dw_w, conf_dw_b, conf_ln_g, conf_ln_b, conf_pw_w, conf_pw_b, ssm_conv_w, ssm_conv_b, ssm_dt_bias, ssm_A_log, ssm_D, ssm_norm_g, w_out_odd, w_up, w_down, w_ple_proj, w_ple_gate):
    bp, tp, d_model = x_prompt.shape
    bs, ts, _ = x_sample.shape
    depth = p_prompt.shape[0]
    n_even, n_odd = (depth + 1) // 2, depth // 2
    inner = ssm_norm_g.shape[-1]
    n_main = w_in_odd.shape[-1] - inner // SSM_HEAD_DIM
    w_in_odd_t = jnp.swapaxes(w_in_odd, 1, 2)
    w_dt = dt_weight(w_in_odd_t, n_main, w_in_odd.shape[-1] - n_main)
    wts = dict(
        w_dt=w_dt,
        g_mix=g_mix, g_mlp=g_mlp, g_ple=g_ple, g_final=g_final,
        pool_w=pool_w.astype(BF16), pool_scale=pool_scale,
        lambda_q1=lambda_q1, lambda_k1=lambda_k1, lambda_q2=lambda_q2, lambda_k2=lambda_k2,
        subln_g=subln_g,
        conf_dw_w=conf_dw_w, conf_dw_b=conf_dw_b, conf_ln_g=conf_ln_g, conf_ln_b=conf_ln_b,
        conf_pw_w=conf_pw_w.astype(BF16), conf_pw_b=conf_pw_b, ssm_conv_w=ssm_conv_w,
        ssm_conv_b=ssm_conv_b, ssm_dt_bias=ssm_dt_bias, ssm_A_log=ssm_A_log, ssm_D=ssm_D,
        ssm_norm_g=ssm_norm_g, w_ple_proj=w_ple_proj.astype(BF16))
    sources = dict(
        w_in=lambda i: (WeightCast(w_in_even, i // 2, w_in_even.shape[-1]) if i % 2 == 0
                        else WeightCast(w_in_odd_t, i // 2, n_main, transposed=True)),
        w_out=lambda i: WeightCast(w_out_even if i % 2 == 0 else w_out_odd, i // 2, d_model),
        w_up=lambda i: WeightCast(w_up, i, w_up.shape[-1]),
        w_down=lambda i: WeightCast(w_down, i, d_model),
        w_gate=lambda i: WeightCast(w_ple_gate, i, d_model))
    ready = {name: [None] * depth for name in sources}
    ready["w_in"][0] = w_in_even[0].astype(BF16)
    ready["w_up"][0] = w_up[0].astype(BF16)
    mmw = MatmulWeights(depth, sources, ready)
    dtype = x_prompt.dtype

    zero_states = dict(
        pool=jnp.zeros((n_even, bp, POOL_BUF, pool_scale.shape[-1]), dtype),
        conf=jnp.zeros((n_odd, bp, CONV_K - 1, conf_dw_b.shape[-1]), dtype),
        mconv=jnp.zeros((n_odd, bp, SSM_CONV_K - 1, ssm_conv_b.shape[-1]), dtype),
        ssm=jnp.zeros((n_odd, bp, inner, SSM_STATE), F32))
    y_p, o_p = _trunk(
        x_prompt.reshape(bp * tp, d_model), p_prompt.reshape(depth, bp * tp, -1).astype(BF16), wts,
        mmw, zero_states, bsz=bp, t_len=tp, t_valid=tp, tm=512, tn=1024, tt=256, paged=None)

    t_pad = SAMPLE_T_PAD
    pad_t = lambda a, axis: jnp.pad(a, [(0, t_pad - ts) if d == axis else (0, 0) for d in range(a.ndim)])
    sample_states = dict(pool=state_pool, conf=state_conf_conv, mconv=state_ssm_conv,
                         ssm=state_ssm.astype(F32).reshape(n_odd, bs, inner, SSM_STATE))
    y_s, o_s = _trunk(
        pad_t(x_sample, 1).reshape(bs * t_pad, d_model),
        pad_t(p_sample, 2).reshape(depth, bs * t_pad, -1).astype(BF16), wts, mmw, sample_states,
        bsz=bs, t_len=t_pad, t_valid=ts, tm=bs * t_pad, tn=2048, tt=t_pad,
        paged=(cache_k, cache_v, page_table))

    def ssm_out(states, bsz):
        return jnp.stack(states).reshape(n_odd, bsz, inner // SSM_HEAD_DIM, SSM_HEAD_DIM,
                                         SSM_STATE).astype(state_ssm.dtype)

    return (y_p.reshape(bp, tp, d_model), y_s.reshape(bs, t_pad, d_model)[:, :ts],
            o_p["k"], o_p["v"], o_s["k"], o_s["v"],
            jnp.stack(o_p["pool"]), jnp.stack(o_s["pool"]),
            jnp.stack(o_p["conf"]), jnp.stack(o_s["conf"]),
            jnp.stack(o_p["mconv"]), jnp.stack(o_s["mconv"]),
            ssm_out(o_p["ssm"], bp), ssm_out(o_s["ssm"], bs))
```

```python
import functools
import math
from typing import NamedTuple

import jax
import jax.numpy as jnp
from jax import lax
from jax.experimental import pallas as pl
from jax.experimental.pallas import tpu as pltpu

F32 = jnp.float32
BF16 = jnp.bfloat16
EPS = 1e-6
NEG_INF = -1e30

V7X_VMEM_BYTES = 64 * 1024 * 1024
VMEM_LIMIT = V7X_VMEM_BYTES - 4 * 1024 * 1024

LANES = 128
SUBLANES = 8
BF16_SUBLANES = 16
POOL_WINDOWS = (2, 4, 8, 16)
POOL_BUF = max(POOL_WINDOWS) - 1
POOL_CARRY = 16
CONV_K = 31
CONV_CARRY = 32
SSM_CONV_K = 4
SSM_CARRY = 8
SSM_CHUNK = 128
DA_HEADS = 8
DA_HEAD_DIM = 128
SSM_HEAD_DIM = 64
SSM_GROUPS = 8
SSM_STATE = 128
SAMPLE_T_PAD = 16


def _cparams(n_axes):
    return pltpu.CompilerParams(
        dimension_semantics=("arbitrary",) * n_axes, vmem_limit_bytes=VMEM_LIMIT)


def _rms_rows(x, g):
    ms = jnp.mean(x * x, axis=-1, keepdims=True)
    return x * lax.rsqrt(ms + EPS) * g


def _sigmoid(x):
    return 0.5 * jnp.tanh(0.5 * x) + 0.5


def _normalise_row_tile(x_hbm, g_ref, xbuf_ref, sem_ref, xn_ref):
    i = pl.program_id(0)
    tm = xn_ref.shape[0]

    def row_copy(tile, slot):
        return pltpu.make_async_copy(x_hbm.at[pl.ds(tile * tm, tm), :], xbuf_ref.at[slot],
                                     sem_ref.at[slot])

    @pl.when(pl.program_id(1) == 0)
    def _():
        slot = lax.rem(i, 2)

        @pl.when(i == 0)
        def _():
            row_copy(0, 0).start()

        row_copy(i, slot).wait()

        @pl.when(i + 1 < pl.num_programs(0))
        def _():
            row_copy(i + 1, 1 - slot).start()

        xn_ref[...] = _rms_rows(xbuf_ref[slot], g_ref[...]).astype(BF16)


def _row_tile_scratch(tm, k):
    return [pltpu.VMEM((2, tm, k), F32), pltpu.SemaphoreType.DMA((2,)), pltpu.VMEM((tm, k), BF16)]


class WeightCast(NamedTuple):
    src: jax.Array
    layer: int
    n_cols: int
    transposed: bool = False


def _cast_specs(cast, grid):
    n_steps = math.prod(grid)

    def tile_fn(n_tiles):
        assert n_tiles <= n_steps

        def tile(*g):
            step = g[0]
            for idx, extent in zip(g[1:], grid[1:]):
                step = step * extent + idx
            return jnp.minimum(step, n_tiles - 1)
        return tile

    if cast.transposed:
        rows = cast.src.shape[2]
        c_blk = LANES * max(1, 1 << math.ceil(math.log2(cast.n_cols / LANES / n_steps)))
        assert cast.n_cols % c_blk == 0
        tile = tile_fn(cast.n_cols // c_blk)
        return (pl.BlockSpec((None, c_blk, rows), lambda *g: (cast.layer, tile(*g), 0)),
                pl.BlockSpec((rows, c_blk), lambda *g: (0, tile(*g))),
                jax.ShapeDtypeStruct((rows, cast.n_cols), BF16))
    rows = cast.src.shape[1]
    r_blk = max(BF16_SUBLANES, 1 << math.ceil(math.log2(rows / n_steps)))
    assert rows % r_blk == 0
    tile = tile_fn(rows // r_blk)
    return (pl.BlockSpec((None, r_blk, cast.n_cols), lambda *g: (cast.layer, tile(*g), 0)),
            pl.BlockSpec((r_blk, cast.n_cols), lambda *g: (tile(*g), 0)),
            jax.ShapeDtypeStruct((rows, cast.n_cols), BF16))


def _dense_call(kernel_fn, grid, in_specs, args, out_spec, out_shape, scratch, cast, name):
    single = not isinstance(out_spec, list)
    out_specs, out_shapes = ([out_spec], [out_shape]) if single else (out_spec, out_shape)
    if cast is not None:
        c_in, c_out, c_shape = _cast_specs(cast, grid)
        in_specs, args = in_specs + [c_in], args + [cast.src]
        out_specs, out_shapes = out_specs + [c_out], out_shapes + [c_shape]
    res = pl.pallas_call(
        functools.partial(kernel_fn, n_in=len(in_specs), has_cast=cast is not None),
        grid=grid, in_specs=in_specs, out_specs=out_specs, out_shape=out_shapes,
        scratch_shapes=scratch, compiler_params=_cparams(len(grid)), name=name)(*args)
    return res[0] if single and cast is None else res


def _split_refs(refs, n_in, has_cast, n_out=1):
    n_refs_out = n_out + (1 if has_cast else 0)
    ins, outs, scratch = refs[:n_in], refs[n_in:n_in + n_refs_out], refs[n_in + n_refs_out:]
    if has_cast:
        src = ins[-1][...]
        if src.shape != outs[-1].shape:
            src = src.T
        outs[-1][...] = src.astype(BF16)
        ins = ins[:-1]
    return ins, (outs[0] if n_out == 1 else tuple(outs[:n_out])), scratch


def _norm_mm_kernel(*refs, n_in, has_cast, act):
    (x_hbm, g_ref, w_ref), o_ref, (xbuf_ref, sem_ref, xn_ref) = _split_refs(refs, n_in, has_cast)
    _normalise_row_tile(x_hbm, g_ref, xbuf_ref, sem_ref, xn_ref)
    acc = jnp.dot(xn_ref[...], w_ref[...], preferred_element_type=F32)
    if act == "relu2":
        acc = jnp.square(jnp.maximum(acc, 0.0))
    o_ref[...] = acc.astype(o_ref.dtype)


def norm_matmul(x, g, w, *, tm, tn, act=None, out_dtype=F32, cast=None):
    m, k = x.shape
    n = w.shape[1]
    assert m % tm == 0 and n % tn == 0
    return _dense_call(
        functools.partial(_norm_mm_kernel, act=act), (m // tm, n // tn),
        [pl.BlockSpec(memory_space=pl.ANY),
         pl.BlockSpec((1, k), lambda i, j: (0, 0)),
         pl.BlockSpec((k, tn), lambda i, j: (0, j))],
        [x, g.reshape(1, k), w],
        pl.BlockSpec((tm, tn), lambda i, j: (i, j)), jax.ShapeDtypeStruct((m, n), out_dtype),
        _row_tile_scratch(tm, k), cast, "norm_matmul")


def _norm_mm2_kernel(*refs, n_in, has_cast):
    (x_hbm, g_ref, w_ref, w2_ref), (o_ref, o2_ref), (xbuf_ref, sem_ref, xn_ref) = _split_refs(
        refs, n_in, has_cast, n_out=2)
    _normalise_row_tile(x_hbm, g_ref, xbuf_ref, sem_ref, xn_ref)
    last = pl.num_programs(1) - 1

    @pl.when(pl.program_id(1) < last)
    def _():
        o_ref[...] = jnp.dot(xn_ref[...], w_ref[...], preferred_element_type=F32)

    @pl.when(pl.program_id(1) == last)
    def _():
        o2_ref[...] = jnp.dot(xn_ref[...], w2_ref[...], preferred_element_type=F32)


def norm_matmul_pair(x, g, w, w2, *, tm, tn, cast=None):
    m, k = x.shape
    n = w.shape[1]
    n2 = w2.shape[1]
    nj = n // tn
    assert m % tm == 0 and n % tn == 0
    res = _dense_call(
        _norm_mm2_kernel, (m // tm, nj + 1),
        [pl.BlockSpec(memory_space=pl.ANY),
         pl.BlockSpec((1, k), lambda i, j: (0, 0)),
         pl.BlockSpec((k, tn), lambda i, j: (0, jnp.minimum(j, nj - 1))),
         pl.BlockSpec((k, n2), lambda i, j: (0, 0))],
        [x, g.reshape(1, k), w, w2],
        [pl.BlockSpec((tm, tn), lambda i, j: (i, jnp.minimum(j, nj - 1))),
         pl.BlockSpec((tm, n2), lambda i, j: (i, 0))],
        [jax.ShapeDtypeStruct((m, n), F32), jax.ShapeDtypeStruct((m, n2), F32)],
        _row_tile_scratch(tm, k), cast, "norm_matmul_pair")
    return (res[:2], res[2]) if cast is not None else res


def _mm2_res_kernel(*refs, n_in, has_cast):
    (a1_ref, a2_ref, w_ref, r_ref), o_ref, _ = _split_refs(refs, n_in, has_cast)
    k1 = a1_ref.shape[1]
    acc = jnp.dot(a1_ref[...], w_ref[:k1, :], preferred_element_type=F32)
    acc += jnp.dot(a2_ref[...], w_ref[k1:, :], preferred_element_type=F32)
    o_ref[...] = r_ref[...] + acc


def matmul2_residual(a1, a2, w, res, *, tm, tn, cast=None):
    m, k1 = a1.shape
    k2 = a2.shape[1]
    n = w.shape[1]
    assert m % tm == 0 and n % tn == 0 and w.shape[0] == k1 + k2
    return _dense_call(
        _mm2_res_kernel, (m // tm, n // tn),
        [pl.BlockSpec((tm, k1), lambda i, j: (i, 0)),
         pl.BlockSpec((tm, k2), lambda i, j: (i, 0)),
         pl.BlockSpec((k1 + k2, tn), lambda i, j: (0, j)),
         pl.BlockSpec((tm, tn), lambda i, j: (i, j))],
        [a1, a2, w, res],
        pl.BlockSpec((tm, tn), lambda i, j: (i, j)), jax.ShapeDtypeStruct((m, n), F32),
        [], cast, "matmul2_residual")


def _mmk_res_kernel(*refs, n_in, has_cast):
    (a_ref, w_ref, r_ref), o_ref, _ = _split_refs(refs, n_in, has_cast)

    @pl.when(pl.program_id(2) == 0)
    def _():
        o_ref[...] = r_ref[...]

    o_ref[...] += jnp.dot(a_ref[...], w_ref[...], preferred_element_type=F32)


def matmul_k_residual(a, w, res, *, tm, tn, tk, cast=None):
    m, k = a.shape
    n = w.shape[1]
    assert m % tm == 0 and n % tn == 0 and k % tk == 0
    return _dense_call(
        _mmk_res_kernel, (m // tm, n // tn, k // tk),
        [pl.BlockSpec((tm, tk), lambda i, j, l: (i, l)),
         pl.BlockSpec((tk, tn), lambda i, j, l: (l, j)),
         pl.BlockSpec((tm, tn), lambda i, j, l: (i, j))],
        [a, w, res],
        pl.BlockSpec((tm, tn), lambda i, j, l: (i, j)), jax.ShapeDtypeStruct((m, n), F32),
        [], cast, "matmul_k_residual")


def _ple_kernel(*refs, n_in, has_cast):
    ((x_hbm, g_ref, wg_ref, pe_ref, wp_ref, r_ref), o_ref,
     (xbuf_ref, sem_ref, xn_ref)) = _split_refs(refs, n_in, has_cast)
    _normalise_row_tile(x_hbm, g_ref, xbuf_ref, sem_ref, xn_ref)
    gate = jax.nn.sigmoid(jnp.dot(xn_ref[...], wg_ref[...], preferred_element_type=F32))
    proj = jnp.dot(pe_ref[...], wp_ref[...], preferred_element_type=F32)
    o_ref[...] = r_ref[...] + gate * proj


def ple_update(h, g, wg, pe, wp, layer, *, tm, tn, cast=None):
    m, k = h.shape
    n = wg.shape[1]
    kp = pe.shape[2]
    assert m % tm == 0 and n % tn == 0
    return _dense_call(
        _ple_kernel, (m // tm, n // tn),
        [pl.BlockSpec(memory_space=pl.ANY),
         pl.BlockSpec((1, k), lambda i, j: (0, 0)),
         pl.BlockSpec((k, tn), lambda i, j: (0, j)),
         pl.BlockSpec((None, tm, kp), lambda i, j: (layer, i, 0)),
         pl.BlockSpec((None, kp, tn), lambda i, j: (layer, 0, j)),
         pl.BlockSpec((tm, tn), lambda i, j: (i, j))],
        [h, g.reshape(1, k), wg, pe, wp, h],
        pl.BlockSpec((tm, tn), lambda i, j: (i, j)), jax.ShapeDtypeStruct((m, n), F32),
        _row_tile_scratch(tm, k), cast, "ple_update")


def _rmsnorm_kernel(x_ref, g_ref, o_ref):
    o_ref[...] = _rms_rows(x_ref[...], g_ref[...])


def rmsnorm(x, g, *, tm):
    m, k = x.shape
    assert m % tm == 0
    return pl.pallas_call(
        _rmsnorm_kernel,
        grid=(m // tm,),
        in_specs=[pl.BlockSpec((tm, k), lambda i: (i, 0)), pl.BlockSpec((1, k), lambda i: (0, 0))],
        out_specs=pl.BlockSpec((tm, k), lambda i: (i, 0)),
        out_shape=jax.ShapeDtypeStruct((m, k), F32),
        compiler_params=_cparams(1),
        name="final_rmsnorm",
    )(x, g.reshape(1, k))


def _pool_kernel(u_ref, buf_ref, w_ref, scale_ref, z_ref, full_ref, *, tt, pos0):
    t = pl.program_id(1)

    @pl.when(t == 0)
    def _():
        full_ref[0:POOL_CARRY, :] = buf_ref[0]

    @pl.when(t != 0)
    def _():
        full_ref[0:POOL_CARRY, :] = full_ref[tt:tt + POOL_CARRY, :]

    full_ref[POOL_CARRY:POOL_CARRY + tt, :] = u_ref[...]
    gc = w_ref.shape[1]
    pos = pos0 + t * tt + lax.broadcasted_iota(jnp.int32, (tt, 1), 0)
    for g, win in enumerate(POOL_WINDOWS):
        cols = slice(g * gc, (g + 1) * gc)
        cur = full_ref[POOL_CARRY:POOL_CARRY + tt, cols]
        tot = cur
        for i in range(1, win):
            tot = tot + full_ref[POOL_CARRY - i:POOL_CARRY - i + tt, cols]
        cnt = jnp.minimum(pos + 1, win).astype(F32)
        d = tot / cnt - cur
        z = jnp.dot(d.astype(BF16), w_ref[g], preferred_element_type=F32)
        z_ref[:, cols] = (z * scale_ref[:, cols]).astype(z_ref.dtype)


def pool_mixer(proj, buf16, w, layer, scale, *, bsz, t_len, tt, pos0):
    c = scale.shape[-1]
    nt = t_len // tt
    assert t_len % tt == 0 and (nt == 1 or tt >= POOL_CARRY)
    return pl.pallas_call(
        functools.partial(_pool_kernel, tt=tt, pos0=pos0),
        grid=(bsz, nt),
        in_specs=[
            pl.BlockSpec((tt, c), lambda b, t: (b * nt + t, 0)),
            pl.BlockSpec((1, POOL_CARRY, c), lambda b, t: (b, 0, 0)),
            pl.BlockSpec((None,) + w.shape[1:], lambda b, t: (layer, 0, 0, 0)),
            pl.BlockSpec((1, c), lambda b, t: (0, 0)),
        ],
        out_specs=pl.BlockSpec((tt, c), lambda b, t: (b * nt + t, 0)),
        out_shape=jax.ShapeDtypeStruct((bsz * t_len, c), BF16),
        scratch_shapes=[pltpu.VMEM((POOL_CARRY + tt, c), F32)],
        compiler_params=_cparams(2),
        name="pool_mixer",
    )(proj, buf16, w, scale.reshape(1, c))


def _lambda_value(lam_ref, lam_init):
    v = lam_ref[...]
    s1 = jnp.sum(v[0:1] * v[1:2], axis=-1, keepdims=True)
    s2 = jnp.sum(v[2:3] * v[3:4], axis=-1, keepdims=True)
    return jnp.exp(s1) - jnp.exp(s2) + lam_init


def _head_slopes(head_idx_f32):
    return jnp.exp2((-8.0 / DA_HEADS) * (head_idx_f32 + 1.0))


def _subln(o, g, lam_init):
    ms = jnp.mean(o * o, axis=-1, keepdims=True)
    return o * lax.rsqrt(ms + EPS) * g * (1.0 - lam_init)


LOG2E = math.log2(math.e)


def _softmax_tile(s, v_bf16, m_ref, l_ref, acc_ref, idx):
    cols = s.shape[1]
    m_prev = m_ref[idx]
    m_next = jnp.maximum(m_prev, jnp.max(s, axis=-1, keepdims=True))
    alpha = jnp.exp2(m_prev - m_next)
    p = jnp.exp2(s - jnp.tile(m_next, (1, cols // LANES)))
    part = p[:, 0:LANES]
    for i in range(1, cols // LANES):
        part = part + p[:, i * LANES:(i + 1) * LANES]
    l_ref[idx] = alpha * l_ref[idx] + part
    acc_ref[idx] = jnp.tile(alpha, (1, acc_ref.shape[-1] // LANES)) * acc_ref[idx] + jnp.dot(
        p.astype(BF16), v_bf16, preferred_element_type=F32)
    m_ref[idx] = m_next


def _attn_prompt_kernel(q_ref, k_ref, v_ref, lam_ref, g_ref, o_ref, kb_ref, vb_ref, m_ref, l_ref,
                        acc_ref, *, tq, lam_init):
    hd = DA_HEAD_DIM
    h = pl.program_id(1)
    qi = pl.program_id(2)

    @pl.when(qi == 0)
    def _():
        kb_ref[...] = k_ref[...].astype(BF16)
        vb_ref[...] = v_ref[...].astype(BF16)

    slope2 = _head_slopes(jnp.full((1, 1), h, jnp.int32).astype(F32)) * LOG2E
    q = (q_ref[...] * (hd ** -0.5 * LOG2E)).astype(BF16)
    m_ref[...] = jnp.full(m_ref.shape, NEG_INF, F32)
    l_ref[...] = jnp.zeros(l_ref.shape, F32)
    acc_ref[...] = jnp.zeros(acc_ref.shape, F32)
    k_local = lax.broadcasted_iota(jnp.int32, (1, tq), 1)

    def tile(ki, diagonal):
        start = pl.multiple_of(ki * tq, tq)
        k = kb_ref[pl.ds(start, tq), :]
        v = vb_ref[pl.ds(start, tq), :]
        col_bias = slope2 * (k_local + ki * tq).astype(F32)
        for c in range(2):
            s = lax.dot_general(q[:, c * hd:(c + 1) * hd], k[:, c * hd:(c + 1) * hd],
                                (((1,), (1,)), ((), ())), preferred_element_type=F32) + col_bias
            if diagonal:
                keep = (lax.broadcasted_iota(jnp.int32, (tq, tq), 0)
                        >= lax.broadcasted_iota(jnp.int32, (tq, tq), 1))
                s = jnp.where(keep, s, NEG_INF)
            _softmax_tile(s, v, m_ref, l_ref, acc_ref, c)

    def body(ki, carry):
        tile(ki, False)
        return carry

    lax.fori_loop(0, qi, body, 0)
    tile(qi, True)
    lam = _lambda_value(lam_ref, lam_init)
    o = (acc_ref[0] / jnp.sum(l_ref[0], axis=-1, keepdims=True)
         - lam * (acc_ref[1] / jnp.sum(l_ref[1], axis=-1, keepdims=True)))
    o_ref[...] = _subln(o, g_ref[...], lam_init).astype(o_ref.dtype)


def attention_prompt(proj, lam_vecs, subln_g, *, bsz, t_len, q_col, k_col, v_col, tq, lam_init):
    hw = 2 * DA_HEAD_DIM
    nq = t_len // tq
    assert t_len % tq == 0
    return pl.pallas_call(
        functools.partial(_attn_prompt_kernel, tq=tq, lam_init=lam_init),
        grid=(bsz, DA_HEADS, nq),
        in_specs=[
            pl.BlockSpec((tq, hw), lambda b, h, i: (b * nq + i, q_col + h)),
            pl.BlockSpec((t_len, hw), lambda b, h, i: (b, k_col + h)),
            pl.BlockSpec((t_len, hw), lambda b, h, i: (b, v_col + h)),
            pl.BlockSpec((4, DA_HEAD_DIM), lambda b, h, i: (0, 0)),
            pl.BlockSpec((1, hw), lambda b, h, i: (0, 0)),
        ],
        out_specs=pl.BlockSpec((tq, hw), lambda b, h, i: (b * nq + i, h)),
        out_shape=jax.ShapeDtypeStruct((bsz * t_len, DA_HEADS * hw), BF16),
        scratch_shapes=[pltpu.VMEM((t_len, hw), BF16), pltpu.VMEM((t_len, hw), BF16),
                        pltpu.VMEM((2, tq, LANES), F32), pltpu.VMEM((2, tq, LANES), F32),
                        pltpu.VMEM((2, tq, hw), F32)],
        compiler_params=_cparams(3),
        name="attention_prompt",
    )(proj, proj, proj, lam_vecs, subln_g.reshape(1, hw))


def _attn_sample_kernel(pt_ref, q_ref, *refs, n_pages, tpad, pos0, lam_init):
    del pt_ref
    k_refs = refs[:n_pages]
    v_refs = refs[n_pages:2 * n_pages]
    kn_ref, vn_ref, lam_ref, g_ref, o_ref, qall_ref, bias_ref, m_ref, l_ref, acc_ref = refs[2 * n_pages:]
    hd = DA_HEAD_DIM
    nh = DA_HEADS
    page = k_refs[0].shape[2]
    nrow = nh * 2 * tpad
    ncol = page * nh
    sidx = pl.program_id(1)
    sh_row = int(math.log2(2 * tpad))
    sh_col = int(math.log2(nh))
    assert 1 << sh_row == 2 * tpad and 1 << sh_col == nh

    def rel_and_match():
        row = lax.broadcasted_iota(jnp.int32, (nrow, ncol), 0)
        col = lax.broadcasted_iota(jnp.int32, (nrow, ncol), 1)
        rel = (jnp.right_shift(col, sh_col) - jnp.bitwise_and(row, tpad - 1)).astype(F32)
        match = jnp.right_shift(row, sh_row) == jnp.bitwise_and(col, nh - 1)
        return rel, match

    def row_slopes(shape):
        row = lax.broadcasted_iota(jnp.int32, shape, 0)
        return _head_slopes(jnp.right_shift(row, sh_row).astype(F32)) * LOG2E

    @pl.when(sidx == 0)
    def _():
        q = q_ref[...] * (hd ** -0.5 * LOG2E)
        lane = lax.broadcasted_iota(jnp.int32, (tpad, 2 * hd), 1)
        for h in range(nh):
            qh = q[:, h * 2 * hd:(h + 1) * 2 * hd]
            qall_ref[h * 2 * tpad:h * 2 * tpad + tpad, :] = jnp.where(lane < hd, qh, 0.0).astype(BF16)
            qall_ref[h * 2 * tpad + tpad:(h + 1) * 2 * tpad, :] = jnp.where(lane >= hd, qh, 0.0).astype(BF16)
        m_ref[...] = jnp.full(m_ref.shape, NEG_INF, F32)
        l_ref[...] = jnp.zeros(l_ref.shape, F32)
        acc_ref[...] = jnp.zeros(acc_ref.shape, F32)
        rel, match = rel_and_match()
        bias_ref[...] = jnp.where(match, row_slopes((nrow, ncol)) * rel, NEG_INF)

    def scores(k_page):
        kk = k_page.reshape(ncol, 2 * hd).astype(BF16)
        return lax.dot_general(qall_ref[...], kk, (((1,), (1,)), ((), ())), preferred_element_type=F32)

    slope_col = row_slopes((nrow, 1))
    for i in range(n_pages):
        base = ((sidx * n_pages + i) * page - pos0).astype(F32)
        s = scores(k_refs[i][0, 0]) + bias_ref[...] + slope_col * base
        vv = v_refs[i][0, 0].reshape(ncol, 2 * hd).astype(BF16)
        _softmax_tile(s, vv, m_ref, l_ref, acc_ref, 0)

    @pl.when(sidx == pl.num_programs(1) - 1)
    def _():
        rel, _ = rel_and_match()
        s = scores(kn_ref[0]) + jnp.where(rel <= 0.0, bias_ref[...], NEG_INF)
        vv = vn_ref[0].reshape(ncol, 2 * hd).astype(BF16)
        _softmax_tile(s, vv, m_ref, l_ref, acc_ref, 0)
        lam = _lambda_value(lam_ref, lam_init)
        a = acc_ref[0] / jnp.sum(l_ref[0], axis=-1, keepdims=True)
        for h in range(nh):
            r0 = h * 2 * tpad
            o = _subln(a[r0:r0 + tpad] - lam * a[r0 + tpad:r0 + 2 * tpad], g_ref[...], lam_init)
            o_ref[:, h * 2 * hd:(h + 1) * 2 * hd] = o.astype(o_ref.dtype)


def attention_sample(proj, cache_k, cache_v, layer, page_table, k_new, v_new, lam_vecs, subln_g,
                     *, bsz, tpad, q_col, n_pages, lam_init):
    hw = 2 * DA_HEAD_DIM
    qw = DA_HEADS * hw
    page = cache_k.shape[2]
    n_past_pages = page_table.shape[1]
    assert n_past_pages % n_pages == 0
    pos0 = n_past_pages * page
    nrow = DA_HEADS * 2 * tpad
    page_block = (1, 1, page, DA_HEADS, hw)

    def page_spec(i):
        return pl.BlockSpec(page_block, lambda b, s, pt: (layer, pt[b, s * n_pages + i], 0, 0, 0))

    new_spec = pl.BlockSpec((1, page, DA_HEADS, hw), lambda b, s, pt: (b, 0, 0, 0))
    grid_spec = pltpu.PrefetchScalarGridSpec(
        num_scalar_prefetch=1,
        grid=(bsz, n_past_pages // n_pages),
        in_specs=[pl.BlockSpec((tpad, qw), lambda b, s, pt: (b, q_col))]
        + [page_spec(i) for i in range(n_pages)] * 2
        + [new_spec, new_spec,
           pl.BlockSpec((4, DA_HEAD_DIM), lambda b, s, pt: (0, 0)),
           pl.BlockSpec((1, hw), lambda b, s, pt: (0, 0))],
        out_specs=pl.BlockSpec((tpad, qw), lambda b, s, pt: (b, 0)),
        scratch_shapes=[pltpu.VMEM((nrow, hw), BF16), pltpu.VMEM((nrow, page * DA_HEADS), F32),
                        pltpu.VMEM((1, nrow, LANES), F32), pltpu.VMEM((1, nrow, LANES), F32),
                        pltpu.VMEM((1, nrow, hw), F32)],
    )
    return pl.pallas_call(
        functools.partial(_attn_sample_kernel, n_pages=n_pages, tpad=tpad, pos0=pos0, lam_init=lam_init),
        grid_spec=grid_spec,
        out_shape=jax.ShapeDtypeStruct((bsz * tpad, qw), BF16),
        compiler_params=_cparams(2),
        name="attention_sample",
    )(page_table, proj, *([cache_k] * n_pages), *([cache_v] * n_pages), k_new, v_new,
      lam_vecs, subln_g.reshape(1, hw))


def _conformer_kernel(a_ref, gate_ref, buf_ref, dww_ref, dwb_ref, lng_ref, lnb_ref, pww_ref, pwb_ref,
                      z_ref, glu_ref, full_ref, cv_ref, *, tt):
    t = pl.program_id(1)

    @pl.when(t == 0)
    def _():
        full_ref[0:CONV_CARRY, :] = buf_ref[0]

    @pl.when(t != 0)
    def _():
        full_ref[0:CONV_CARRY, :] = full_ref[tt:tt + CONV_CARRY, :]

    glu = a_ref[...] * _sigmoid(gate_ref[...])
    full_ref[CONV_CARRY:CONV_CARRY + tt, :] = glu
    glu_ref[...] = glu
    first = CONV_CARRY - (CONV_K - 1)
    rb = min(tt, 128)
    phases = {}
    for k in range(CONV_K):
        phases.setdefault((first + k) % SUBLANES, []).append(k)

    def lane_chunk(c, carry):
        col = pl.ds(pl.multiple_of(c * LANES, LANES), LANES)
        for r0 in range(0, tt, rb):
            acc = jnp.zeros((rb, LANES), F32) + dwb_ref[:, col]
            for phase, taps in phases.items():
                rows = rb if phase == 0 else rb + SUBLANES
                part = None
                for k in taps:
                    base = r0 + first + k - phase
                    term = dww_ref[k:k + 1, col] * full_ref[base:base + rows, col]
                    part = term if part is None else part + term
                acc = acc + part[phase:phase + rb]
            cv_ref[r0:r0 + rb, col] = acc
        return carry

    lax.fori_loop(0, cv_ref.shape[1] // LANES, lane_chunk, 0)
    y = cv_ref[...]
    mu = jnp.mean(y, axis=-1, keepdims=True)
    yc = y - mu
    var = jnp.mean(yc * yc, axis=-1, keepdims=True)
    y = yc * lax.rsqrt(var + EPS) * lng_ref[...] + lnb_ref[...]
    y = y * _sigmoid(y)
    z = jnp.dot(y.astype(BF16), pww_ref[...], preferred_element_type=F32) + pwb_ref[...]
    z_ref[...] = z.astype(z_ref.dtype)


def conformer_mixer(proj, buf32, dw_w, dw_b, ln_g, ln_b, pw_w, layer, pw_b, *, bsz, t_len, tt):
    c = dw_b.shape[-1]
    nt = t_len // tt
    assert t_len % tt == 0 and (nt == 1 or tt >= CONV_CARRY)
    vec = lambda: pl.BlockSpec((1, c), lambda b, t: (0, 0))
    rows = lambda j: pl.BlockSpec((tt, c), lambda b, t: (b * nt + t, j))
    return pl.pallas_call(
        functools.partial(_conformer_kernel, tt=tt),
        grid=(bsz, nt),
        in_specs=[
            rows(0), rows(1),
            pl.BlockSpec((1, CONV_CARRY, c), lambda b, t: (b, 0, 0)),
            pl.BlockSpec((CONV_K, c), lambda b, t: (0, 0)),
            vec(), vec(), vec(),
            pl.BlockSpec((None, c, c), lambda b, t: (layer, 0, 0)),
            vec(),
        ],
        out_specs=[rows(0), rows(0)],
        out_shape=[jax.ShapeDtypeStruct((bsz * t_len, c), BF16),
                   jax.ShapeDtypeStruct((bsz * t_len, c), F32)],
        scratch_shapes=[pltpu.VMEM((CONV_CARRY + tt, c), F32), pltpu.VMEM((tt, c), F32)],
        compiler_params=_cparams(2),
        name="conformer_mixer",
    )(proj, proj, buf32, dw_w, dw_b.reshape(1, c), ln_g.reshape(1, c), ln_b.reshape(1, c),
      pw_w, pw_b.reshape(1, c))


def _split3(x, axis):
    hi = x.astype(BF16)
    rest = x - hi.astype(F32)
    mid = rest.astype(BF16)
    lo = (rest - mid.astype(F32)).astype(BF16)
    return jnp.concatenate([hi, mid, lo], axis=axis)


def _select_dot_left(sel_bf16, x):
    n = x.shape[1]
    r = jnp.dot(sel_bf16, _split3(x, 1), preferred_element_type=F32)
    return r[:, 0:n] + r[:, n:2 * n] + r[:, 2 * n:3 * n]


def _ssd_kernel(z_ref, x_ref, b_ref, c_ref, dt_ref, buf_ref, cw_ref, cb_ref, dtb_ref, alog_ref,
                dexp_ref, ng_ref, e_ref, et_ref, h0_ref, y_ref, hn_ref, full_ref, st_ref,
                *, t_valid):
    t = pl.program_id(1)
    ln = x_ref.shape[0]
    inner = x_ref.shape[1]
    gn = b_ref.shape[1]
    ns = SSM_STATE
    gw = inner // SSM_GROUPS
    hpg = gw // SSM_HEAD_DIM

    @pl.when(t == 0)
    def _():
        full_ref[0:SSM_CARRY, :] = buf_ref[0]
        st_ref[...] = h0_ref[0]

    @pl.when(t != 0)
    def _():
        full_ref[0:SSM_CARRY, :] = full_ref[ln:ln + SSM_CARRY, :]

    full_ref[SSM_CARRY:SSM_CARRY + ln, 0:inner] = x_ref[...]
    full_ref[SSM_CARRY:SSM_CARRY + ln, inner:inner + gn] = b_ref[...]
    full_ref[SSM_CARRY:SSM_CARRY + ln, inner + gn:inner + 2 * gn] = c_ref[...]
    first = SSM_CARRY - (SSM_CONV_K - 1)
    xc = cb_ref[...] + cw_ref[0:1, :] * full_ref[first:first + ln, :]
    for k in range(1, SSM_CONV_K):
        xc = xc + cw_ref[k:k + 1, :] * full_ref[first + k:first + k + ln, :]
    xc = xc * _sigmoid(xc)
    xs = xc[:, 0:inner]
    bm = xc[:, inner:inner + gn].astype(BF16)
    cm = xc[:, inner + gn:inner + 2 * gn].astype(BF16)

    dt_in = dt_ref[...] + dtb_ref[...]
    dt = jnp.maximum(dt_in, 0.0) + jnp.log1p(jnp.exp(-jnp.abs(dt_in)))
    row = lax.broadcasted_iota(jnp.int32, (ln, ln), 0)
    col = lax.broadcasted_iota(jnp.int32, (ln, ln), 1)
    if t_valid < ln:
        dt = jnp.where(lax.broadcasted_iota(jnp.int32, dt.shape, 0) < t_valid, dt, 0.0)
    ad = dt * (-jnp.exp(alog_ref[...]))
    causal = row >= col
    a_cs = _select_dot_left(jnp.where(causal, 1.0, 0.0).astype(BF16), ad)
    a_cs_t = a_cs.T
    a_last = a_cs[ln - 1:ln, :]
    per_head = jnp.concatenate([dt, jnp.exp(a_last - a_cs), jnp.exp(a_cs)], axis=0)
    per_chan = jnp.dot(_split3(per_head, 1), e_ref[...], preferred_element_type=F32)
    xd = xs * per_chan[0:ln]
    xdw = (xd * per_chan[ln:2 * ln]).astype(BF16)
    xd = xd.astype(BF16)
    exp_acs = per_chan[2 * ln:3 * ln]
    last_t = jnp.broadcast_to(jnp.exp(a_cs_t[:, ln - 1:ln]), a_cs_t.shape)
    st_decay = _select_dot_left(et_ref[...], last_t)

    lane_head = lax.broadcasted_iota(jnp.int32, (ln, gw), 1) // SSM_HEAD_DIM
    for g in range(SSM_GROUPS):
        cols = slice(g * gw, (g + 1) * gw)
        scols = slice(g * ns, (g + 1) * ns)
        scores = lax.dot_general(cm[:, scols], bm[:, scols], (((1,), (1,)), ((), ())),
                                 preferred_element_type=F32)
        lhs, rhs = [], []
        for r in range(hpg):
            h = g * hpg + r
            seg = a_cs[:, h:h + 1] - a_cs_t[h:h + 1, :]
            lhs.append((scores * jnp.where(causal, jnp.exp(seg), 0.0)).astype(BF16))
            rhs.append(jnp.where(lane_head == r, xd[:, cols], jnp.zeros_like(xd[:, cols])))
        y = jnp.dot(jnp.concatenate(lhs, axis=1), jnp.concatenate(rhs, axis=0),
                    preferred_element_type=F32)
        h_prev = st_ref[cols, :]
        y = y + exp_acs[:, cols] * lax.dot_general(
            cm[:, scols], h_prev.astype(BF16), (((1,), (1,)), ((), ())), preferred_element_type=F32)
        new_states = lax.dot_general(xdw[:, cols], bm[:, scols], (((0,), (0,)), ((), ())),
                                     preferred_element_type=F32)
        st_ref[cols, :] = h_prev * st_decay[cols, :] + new_states
        y = y + dexp_ref[:, cols] * xs[:, cols]
        zg = z_ref[:, cols]
        y = y * (zg * _sigmoid(zg))
        y_ref[:, cols] = (_rms_rows(y, ng_ref[:, cols])).astype(y_ref.dtype)

    @pl.when(t == pl.num_programs(1) - 1)
    def _():
        hn_ref[0] = st_ref[...]


def ssd_mixer(proj, dt_raw, buf8, h0, conv_w, conv_b, dt_bias, a_log, d_skip, norm_g,
              *, bsz, t_len, t_valid, z_col, x_col, b_col, c_col):
    inner = norm_g.shape[-1]
    heads = inner // SSM_HEAD_DIM
    gn = SSM_GROUPS * SSM_STATE
    xbc = inner + 2 * gn
    ln = SSM_CHUNK
    nc = t_len // ln
    assert t_len % ln == 0 and heads <= LANES

    def pad_heads(v):
        return jnp.pad(v.astype(F32), (0, LANES - heads)).reshape(1, LANES)

    expand = (jnp.arange(LANES)[:, None] == (jnp.arange(inner) // SSM_HEAD_DIM)[None, :]).astype(BF16)
    d_exp = jnp.repeat(d_skip.astype(F32), SSM_HEAD_DIM).reshape(1, inner)
    const = lambda shape: pl.BlockSpec(shape, lambda b, t: (0,) * len(shape))
    rows = lambda w, j: pl.BlockSpec((ln, w), lambda b, t: (b * nc + t, j))
    return pl.pallas_call(
        functools.partial(_ssd_kernel, t_valid=t_valid),
        grid=(bsz, nc),
        in_specs=[
            rows(inner, z_col), rows(inner, x_col), rows(gn, b_col), rows(gn, c_col),
            rows(LANES, 0),
            pl.BlockSpec((1, SSM_CARRY, xbc), lambda b, t: (b, 0, 0)),
            const((SSM_CONV_K, xbc)), const((1, xbc)),
            const((1, LANES)), const((1, LANES)),
            const((1, inner)), const((1, inner)),
            const((3 * LANES, inner)), const((inner, LANES)),
            pl.BlockSpec((1, inner, SSM_STATE), lambda b, t: (b, 0, 0)),
        ],
        out_specs=[rows(inner, 0), pl.BlockSpec((1, inner, SSM_STATE), lambda b, t: (b, 0, 0))],
        out_shape=[jax.ShapeDtypeStruct((bsz * t_len, inner), BF16),
                   jax.ShapeDtypeStruct((bsz, inner, SSM_STATE), F32)],
        scratch_shapes=[pltpu.VMEM((SSM_CARRY + ln, xbc), F32), pltpu.VMEM((inner, SSM_STATE), F32)],
        compiler_params=_cparams(2),
        name="ssd_mixer",
    )(proj, proj, proj, proj, dt_raw, buf8, conv_w, conv_b.reshape(1, xbc), pad_heads(dt_bias),
      pad_heads(a_log), d_exp, norm_g.reshape(1, inner), jnp.tile(expand, (3, 1)), expand.T, h0)


def _kv_layout_kernel(*refs, n_layers):
    ins = refs[:2 * n_layers]
    ko_ref, vo_ref = refs[2 * n_layers:]
    hw = ko_ref.shape[-1]
    for j in range(n_layers):
        @pl.when(pl.program_id(0) == j)
        def _(j=j):
            for h in range(DA_HEADS):
                ko_ref[0, 0, :, h, :] = ins[2 * j][:, h * hw:(h + 1) * hw]
                vo_ref[0, 0, :, h, :] = ins[2 * j + 1][:, h * hw:(h + 1) * hw]


def kv_layout(projs, *, bsz, t_len, tt, k_col, v_col):
    n_layers = len(projs)
    hw = 2 * DA_HEAD_DIM
    width = DA_HEADS * hw
    nt = t_len // tt
    assert t_len % tt == 0

    def src(j, col):
        return pl.BlockSpec((tt, width), lambda l, b, t: (jnp.where(l == j, b * nt + t, 0), col))

    out_spec = pl.BlockSpec((1, 1, tt, DA_HEADS, hw), lambda l, b, t: (l, b, t, 0, 0))
    out_shape = jax.ShapeDtypeStruct((n_layers, bsz, t_len, DA_HEADS, hw), F32)
    in_specs, args = [], []
    for j, p in enumerate(projs):
        in_specs += [src(j, k_col), src(j, v_col)]
        args += [p, p]
    return pl.pallas_call(
        functools.partial(_kv_layout_kernel, n_layers=n_layers),
        grid=(n_layers, bsz, nt),
        in_specs=in_specs,
        out_specs=[out_spec, out_spec],
        out_shape=[out_shape, out_shape],
        compiler_params=_cparams(3),
        name="kv_layout",
    )(*args)


def _tail_rows(buf, rows, cols, t_valid, n):
    take = min(n, t_valid)
    new = rows[:, t_valid - take:t_valid, cols]
    if take == n:
        return new
    return jnp.concatenate([buf[:, buf.shape[1] - (n - take):], new], axis=1)


def _pad_front(buf, rows):
    return jnp.pad(buf, ((0, 0), (rows - buf.shape[1], 0), (0, 0)))


def _dt_weight_kernel(src_ref, o_ref):
    rows = src_ref[...]
    pad = jnp.zeros((LANES - rows.shape[0], rows.shape[1]), F32)
    o_ref[...] = jnp.concatenate([rows, pad], axis=0).T.astype(BF16)


def dt_weight(w_t, row0, n_rows):
    n_layers, _, k = w_t.shape
    assert row0 % n_rows == 0 and n_rows % SUBLANES == 0 and n_rows <= LANES
    return pl.pallas_call(
        _dt_weight_kernel,
        grid=(n_layers,),
        in_specs=[pl.BlockSpec((None, n_rows, k), lambda l: (l, row0 // n_rows, 0))],
        out_specs=pl.BlockSpec((None, k, LANES), lambda l: (l, 0, 0)),
        out_shape=jax.ShapeDtypeStruct((n_layers, k, LANES), BF16),
        compiler_params=_cparams(1),
        name="dt_weight",
    )(w_t)


class MatmulWeights:
    def __init__(self, depth, sources, ready):
        self.depth = depth
        self._sources = sources
        self.ready = ready

    def get(self, name, layer):
        return self.ready[name][layer]

    def pending(self, name, layer):
        if layer >= self.depth or self.ready[name][layer] is not None:
            return None
        return self._sources[name](layer)

    def run(self, fn, target, *args, **kwargs):
        cast = self.pending(*target) if target is not None else None
        res = fn(*args, cast=cast, **kwargs)
        if cast is None:
            return res
        out, self.ready[target[0]][target[1]] = res
        return out


def _trunk(x, pemb, wts, mmw, states, *, bsz, t_len, t_valid, tm, tn, tt, paged):
    depth = pemb.shape[0]
    d_model = x.shape[1]
    pool_w = d_model // 2
    da_w = DA_HEADS * 2 * DA_HEAD_DIM
    hw = 2 * DA_HEAD_DIM
    conv_w = d_model // 2
    inner = d_model // 2
    gn = SSM_GROUPS * SSM_STATE
    xbc_w = inner + 2 * gn
    tm2 = min(2 * tm, bsz * t_len)
    pos0 = 0 if paged is None else paged[2].shape[1] * paged[0].shape[2]
    h = x
    outs = dict(k=[], v=[], pool=[], conf=[], mconv=[], ssm=[])
    even_projs = []
    q_col = pool_w // hw

    def pad_seq_rows(a, t_new):
        return jnp.pad(a.reshape(bsz, t_len, -1),
                       ((0, 0), (0, t_new - t_len), (0, 0))).reshape(bsz * t_new, -1)

    for i in range(depth):
        j = i // 2
        if i % 2 == 0:
            proj = mmw.run(norm_matmul, ("w_out", i), h, wts["g_mix"][i], mmw.get("w_in", i),
                           tm=tm, tn=tn)
            rows = proj.reshape(bsz, t_len, -1)
            pool_buf = states["pool"][j]
            outs["pool"].append(_tail_rows(pool_buf, rows, slice(0, pool_w), t_valid, POOL_BUF))
            z_pool = pool_mixer(proj, _pad_front(pool_buf, POOL_CARRY), wts["pool_w"], j,
                                wts["pool_scale"][j], bsz=bsz, t_len=t_len, tt=tt, pos0=pos0)
            lam_init = 0.8 - 0.6 * math.exp(-0.3 * i)
            lam_vecs = jnp.stack([wts["lambda_q1"][j], wts["lambda_k1"][j],
                                  wts["lambda_q2"][j], wts["lambda_k2"][j]]).astype(F32)
            if paged is None:
                even_projs.append(proj)
                attn = attention_prompt(
                    proj, lam_vecs, wts["subln_g"][j], bsz=bsz, t_len=t_len, q_col=q_col,
                    k_col=q_col + DA_HEADS, v_col=q_col + 2 * DA_HEADS, tq=512, lam_init=lam_init)
            else:
                cache_k, cache_v, page_table = paged
                page = cache_k.shape[2]
                k_seq = rows[:, :t_valid, pool_w + da_w:pool_w + 2 * da_w].reshape(
                    bsz, t_valid, DA_HEADS, hw)
                v_seq = rows[:, :t_valid, pool_w + 2 * da_w:pool_w + 3 * da_w].reshape(
                    bsz, t_valid, DA_HEADS, hw)
                outs["k"].append(k_seq)
                outs["v"].append(v_seq)
                new_page = lambda r: jnp.pad(r, ((0, 0), (0, page - t_valid), (0, 0), (0, 0)))
                attn = attention_sample(
                    proj, cache_k, cache_v, j, page_table, new_page(k_seq), new_page(v_seq),
                    lam_vecs, wts["subln_g"][j], bsz=bsz, tpad=t_len, q_col=pool_w // da_w,
                    n_pages=8, lam_init=lam_init)
            h = mmw.run(matmul2_residual, ("w_gate", i), z_pool, attn, mmw.get("w_out", i), h,
                        tm=tm2, tn=tn // 2)
        else:
            x0 = 2 * conv_w + inner
            proj, dt_raw = mmw.run(norm_matmul_pair, ("w_out", i), h, wts["g_mix"][i],
                                   mmw.get("w_in", i), wts["w_dt"][j], tm=tm, tn=tn)
            rows = proj.reshape(bsz, t_len, -1)
            conf_buf = states["conf"][j]
            zc, glu = conformer_mixer(
                proj, _pad_front(conf_buf, CONV_CARRY), wts["conf_dw_w"][j], wts["conf_dw_b"][j],
                wts["conf_ln_g"][j], wts["conf_ln_b"][j], wts["conf_pw_w"], j, wts["conf_pw_b"][j],
                bsz=bsz, t_len=t_len, tt=tt)
            outs["conf"].append(_tail_rows(conf_buf, glu.reshape(bsz, t_len, conv_w),
                                           slice(None), t_valid, CONV_K - 1))
            mconv_buf = states["mconv"][j]
            outs["mconv"].append(_tail_rows(mconv_buf, rows, slice(x0, x0 + xbc_w), t_valid,
                                            SSM_CONV_K - 1))
            t_ssd = -(-t_len // SSM_CHUNK) * SSM_CHUNK
            proj_ssd = proj
            if t_ssd != t_len:
                proj_ssd = pad_seq_rows(proj, t_ssd)
                dt_raw = pad_seq_rows(dt_raw, t_ssd)
            y, h_new = ssd_mixer(
                proj_ssd, dt_raw, _pad_front(mconv_buf, SSM_CARRY), states["ssm"][j],
                wts["ssm_conv_w"][j], wts["ssm_conv_b"][j], wts["ssm_dt_bias"][j],
                wts["ssm_A_log"][j], wts["ssm_D"][j], wts["ssm_norm_g"][j],
                bsz=bsz, t_len=t_ssd, t_valid=min(t_valid, SSM_CHUNK),
                z_col=2 * conv_w // inner, x_col=x0 // inner, b_col=(x0 + inner) // gn,
                c_col=(x0 + inner + gn) // gn)
            if t_ssd != t_len:
                y = y.reshape(bsz, t_ssd, inner)[:, :t_len].reshape(bsz * t_len, inner)
            outs["ssm"].append(h_new)
            h = mmw.run(matmul2_residual, ("w_gate", i), zc, y, mmw.get("w_out", i), h,
                        tm=tm2, tn=tn // 2)
        hidden = mmw.run(norm_matmul, ("w_down", i), h, wts["g_mlp"][i], mmw.get("w_up", i),
                         tm=tm, tn=tn, act="relu2", out_dtype=BF16)
        h = mmw.run(matmul_k_residual, ("w_up", i + 1), hidden, mmw.get("w_down", i), h,
                    tm=tm2, tn=tn, tk=2 * tn)
        h = mmw.run(ple_update, ("w_in", i + 1), h, wts["g_ple"][i], mmw.get("w_gate", i), pemb,
                    wts["w_ple_proj"], i, tm=tm, tn=tn // 2)
    if paged is None:
        outs["k"], outs["v"] = kv_layout(even_projs, bsz=bsz, t_len=t_len, tt=tt,
                                         k_col=pool_w // da_w + 1, v_col=pool_w // da_w + 2)
    else:
        outs["k"], outs["v"] = jnp.stack(outs["k"]), jnp.stack(outs["v"])
    y = rmsnorm(h, wts["g_final"], tm=min(256, bsz * t_len))
    return y, outs


def kernel(x_prompt, x_sample, p_prompt, p_sample, cache_k, cache_v, page_table, state_pool, state_conf_conv, state_ssm_conv, state_ssm, g_mix, g_mlp, g_ple, g_final, w_in_even, pool_w, pool_scale, lambda_q1, lambda_k1, lambda_q2, lambda_k2, subln_g, w_out_even, w_in_odd, conf_dw_w, conf_dw_b, conf_ln_g, conf_ln_b, conf_pw_w, conf_pw_b, ssm_conv_w, ssm_conv_b, ssm_dt_bias, ssm_A_log, ssm_D, ssm_norm_g, w_out_odd, w_up, w_down, w_ple_proj, w_ple_gate):
    bp, tp, d_model = x_prompt.shape
    bs, ts, _ = x_sample.shape
    depth = p_prompt.shape[0]
    n_even, n_odd = (depth + 1) // 2, depth // 2
    inner = ssm_norm_g.shape[-1]
    n_main = w_in_odd.shape[-1] - inner // SSM_HEAD_DIM
    w_in_odd_t = jnp.swapaxes(w_in_odd, 1, 2)
    w_dt = dt_weight(w_in_odd_t, n_main, w_in_odd.shape[-1] - n_main)
    wts = dict(
        w_dt=w_dt,
        g_mix=g_mix, g_mlp=g_mlp, g_ple=g_ple, g_final=g_final,
        pool_w=pool_w.astype(BF16), pool_scale=pool_scale,
        lambda_q1=lambda_q1, lambda_k1=lambda_k1, lambda_q2=lambda_q2, lambda_k2=lambda_k2,
        subln_g=subln_g,
        conf_dw_w=conf_dw_w, conf_dw_b=conf_dw_b, conf_ln_g=conf_ln_g, conf_ln_b=conf_ln_b,
        conf_pw_w=conf_pw_w.astype(BF16), conf_pw_b=conf_pw_b, ssm_conv_w=ssm_conv_w,
        ssm_conv_b=ssm_conv_b, ssm_dt_bias=ssm_dt_bias, ssm_A_log=ssm_A_log, ssm_D=ssm_D,
        ssm_norm_g=ssm_norm_g, w_ple_proj=w_ple_proj.astype(BF16))
    sources = dict(
        w_in=lambda i: (WeightCast(w_in_even, i // 2, w_in_even.shape[-1]) if i % 2 == 0
                        else WeightCast(w_in_odd_t, i // 2, n_main, transposed=True)),
        w_out=lambda i: WeightCast(w_out_even if i % 2 == 0 else w_out_odd, i // 2, d_model),
        w_up=lambda i: WeightCast(w_up, i, w_up.shape[-1]),
        w_down=lambda i: WeightCast(w_down, i, d_model),
        w_gate=lambda i: WeightCast(w_ple_gate, i, d_model))
    ready = {name: [None] * depth for name in sources}
    ready["w_in"][0] = w_in_even[0].astype(BF16)
    ready["w_up"][0] = w_up[0].astype(BF16)
    mmw = MatmulWeights(depth, sources, ready)
    dtype = x_prompt.dtype

    zero_states = dict(
        pool=jnp.zeros((n_even, bp, POOL_BUF, pool_scale.shape[-1]), dtype),
        conf=jnp.zeros((n_odd, bp, CONV_K - 1, conf_dw_b.shape[-1]), dtype),
        mconv=jnp.zeros((n_odd, bp, SSM_CONV_K - 1, ssm_conv_b.shape[-1]), dtype),
        ssm=jnp.zeros((n_odd, bp, inner, SSM_STATE), F32))
    y_p, o_p = _trunk(
        x_prompt.reshape(bp * tp, d_model), p_prompt.reshape(depth, bp * tp, -1).astype(BF16), wts,
        mmw, zero_states, bsz=bp, t_len=tp, t_valid=tp, tm=512, tn=1024, tt=256, paged=None)

    t_pad = SAMPLE_T_PAD
    pad_t = lambda a, axis: jnp.pad(a, [(0, t_pad - ts) if d == axis else (0, 0) for d in range(a.ndim)])
    sample_states = dict(pool=state_pool, conf=state_conf_conv, mconv=state_ssm_conv,
                         ssm=state_ssm.astype(F32).reshape(n_odd, bs, inner, SSM_STATE))
    y_s, o_s = _trunk(
        pad_t(x_sample, 1).reshape(bs * t_pad, d_model),
        pad_t(p_sample, 2).reshape(depth, bs * t_pad, -1).astype(BF16), wts, mmw, sample_states,
        bsz=bs, t_len=t_pad, t_valid=ts, tm=bs * t_pad, tn=2048, tt=t_pad,
        paged=(cache_k, cache_v, page_table))

    def ssm_out(states, bsz):
        return jnp.stack(states).reshape(n_odd, bsz, inner // SSM_HEAD_DIM, SSM_HEAD_DIM,
                                         SSM_STATE).astype(state_ssm.dtype)

    return (y_p.reshape(bp, tp, d_model), y_s.reshape(bs, t_pad, d_model)[:, :ts],
            o_p["k"], o_p["v"], o_s["k"], o_s["v"],
            jnp.stack(o_p["pool"]), jnp.stack(o_s["pool"]),
            jnp.stack(o_p["conf"]), jnp.stack(o_s["conf"]),
            jnp.stack(o_p["mconv"]), jnp.stack(o_s["mconv"]),
            ssm_out(o_p["ssm"], bp), ssm_out(o_s["ssm"], bs))
```

```python
import functools
import math
from typing import NamedTuple

import jax
import jax.numpy as jnp
from jax import lax
from jax.experimental import pallas as pl
from jax.experimental.pallas import tpu as pltpu

F32 = jnp.float32
BF16 = jnp.bfloat16
EPS = 1e-6
NEG_INF = -1e30

V7X_VMEM_BYTES = 64 * 1024 * 1024
VMEM_LIMIT = V7X_VMEM_BYTES - 4 * 1024 * 1024

LANES = 128
SUBLANES = 8
BF16_SUBLANES = 16
POOL_WINDOWS = (2, 4, 8, 16)
POOL_BUF = max(POOL_WINDOWS) - 1
POOL_CARRY = 16
CONV_K = 31
CONV_CARRY = 32
SSM_CONV_K = 4
SSM_CARRY = 8
SSM_CHUNK = 128
DA_HEADS = 8
DA_HEAD_DIM = 128
SSM_HEAD_DIM = 64
SSM_GROUPS = 8
SSM_STATE = 128
SAMPLE_T_PAD = 16


def _cparams(n_axes):
    return pltpu.CompilerParams(
        dimension_semantics=("arbitrary",) * n_axes, vmem_limit_bytes=VMEM_LIMIT)


def _rms_rows(x, g):
    ms = jnp.mean(x * x, axis=-1, keepdims=True)
    return x * lax.rsqrt(ms + EPS) * g


def _sigmoid(x):
    return 0.5 * jnp.tanh(0.5 * x) + 0.5


def _normalise_row_tile(x_hbm, g_ref, xbuf_ref, sem_ref, xn_ref):
    i = pl.program_id(0)
    tm = xn_ref.shape[0]

    def row_copy(tile, slot):
        return pltpu.make_async_copy(x_hbm.at[pl.ds(tile * tm, tm), :], xbuf_ref.at[slot],
                                     sem_ref.at[slot])

    @pl.when(pl.program_id(1) == 0)
    def _():
        slot = lax.rem(i, 2)

        @pl.when(i == 0)
        def _():
            row_copy(0, 0).start()

        row_copy(i, slot).wait()

        @pl.when(i + 1 < pl.num_programs(0))
        def _():
            row_copy(i + 1, 1 - slot).start()

        xn_ref[...] = _rms_rows(xbuf_ref[slot], g_ref[...]).astype(BF16)


def _row_tile_scratch(tm, k):
    return [pltpu.VMEM((2, tm, k), F32), pltpu.SemaphoreType.DMA((2,)), pltpu.VMEM((tm, k), BF16)]


class WeightCast(NamedTuple):
    src: jax.Array
    layer: int
    n_cols: int
    transposed: bool = False


def _cast_specs(cast, grid):
    n_steps = math.prod(grid)

    def tile_fn(n_tiles):
        assert n_tiles <= n_steps

        def tile(*g):
            step = g[0]
            for idx, extent in zip(g[1:], grid[1:]):
                step = step * extent + idx
            return jnp.minimum(step, n_tiles - 1)
        return tile

    if cast.transposed:
        rows = cast.src.shape[2]
        c_blk = LANES * max(1, 1 << math.ceil(math.log2(cast.n_cols / LANES / n_steps)))
        assert cast.n_cols % c_blk == 0
        tile = tile_fn(cast.n_cols // c_blk)
        return (pl.BlockSpec((None, c_blk, rows), lambda *g: (cast.layer, tile(*g), 0)),
                pl.BlockSpec((rows, c_blk), lambda *g: (0, tile(*g))),
                jax.ShapeDtypeStruct((rows, cast.n_cols), BF16))
    rows = cast.src.shape[1]
    r_blk = max(BF16_SUBLANES, 1 << math.ceil(math.log2(rows / n_steps)))
    assert rows % r_blk == 0
    tile = tile_fn(rows // r_blk)
    return (pl.BlockSpec((None, r_blk, cast.n_cols), lambda *g: (cast.layer, tile(*g), 0)),
            pl.BlockSpec((r_blk, cast.n_cols), lambda *g: (tile(*g), 0)),
            jax.ShapeDtypeStruct((rows, cast.n_cols), BF16))


def _dense_call(kernel_fn, grid, in_specs, args, out_spec, out_shape, scratch, cast, name):
    out_specs, out_shapes = [out_spec], [out_shape]
    if cast is not None:
        c_in, c_out, c_shape = _cast_specs(cast, grid)
        in_specs, args = in_specs + [c_in], args + [cast.src]
        out_specs, out_shapes = out_specs + [c_out], out_shapes + [c_shape]
    res = pl.pallas_call(
        functools.partial(kernel_fn, n_in=len(in_specs), has_cast=cast is not None),
        grid=grid, in_specs=in_specs, out_specs=out_specs, out_shape=out_shapes,
        scratch_shapes=scratch, compiler_params=_cparams(len(grid)), name=name)(*args)
    return res if cast is not None else res[0]


def _split_refs(refs, n_in, has_cast):
    n_out = 2 if has_cast else 1
    ins, outs, scratch = refs[:n_in], refs[n_in:n_in + n_out], refs[n_in + n_out:]
    if has_cast:
        src = ins[-1][...]
        if src.shape != outs[1].shape:
            src = src.T
        outs[1][...] = src.astype(BF16)
        ins = ins[:-1]
    return ins, outs[0], scratch


def _norm_mm_kernel(*refs, n_in, has_cast, act):
    (x_hbm, g_ref, w_ref), o_ref, (xbuf_ref, sem_ref, xn_ref) = _split_refs(refs, n_in, has_cast)
    _normalise_row_tile(x_hbm, g_ref, xbuf_ref, sem_ref, xn_ref)
    acc = jnp.dot(xn_ref[...], w_ref[...], preferred_element_type=F32)
    if act == "relu2":
        acc = jnp.square(jnp.maximum(acc, 0.0))
    o_ref[...] = acc.astype(o_ref.dtype)


def norm_matmul(x, g, w, *, tm, tn, act=None, out_dtype=F32, cast=None):
    m, k = x.shape
    n = w.shape[1]
    assert m % tm == 0 and n % tn == 0
    return _dense_call(
        functools.partial(_norm_mm_kernel, act=act), (m // tm, n // tn),
        [pl.BlockSpec(memory_space=pl.ANY),
         pl.BlockSpec((1, k), lambda i, j: (0, 0)),
         pl.BlockSpec((k, tn), lambda i, j: (0, j))],
        [x, g.reshape(1, k), w],
        pl.BlockSpec((tm, tn), lambda i, j: (i, j)), jax.ShapeDtypeStruct((m, n), out_dtype),
        _row_tile_scratch(tm, k), cast, "norm_matmul")


def _mm2_res_kernel(*refs, n_in, has_cast):
    (a1_ref, a2_ref, w_ref, r_ref), o_ref, _ = _split_refs(refs, n_in, has_cast)
    k1 = a1_ref.shape[1]
    acc = jnp.dot(a1_ref[...], w_ref[:k1, :], preferred_element_type=F32)
    acc += jnp.dot(a2_ref[...], w_ref[k1:, :], preferred_element_type=F32)
    o_ref[...] = r_ref[...] + acc


def matmul2_residual(a1, a2, w, res, *, tm, tn, cast=None):
    m, k1 = a1.shape
    k2 = a2.shape[1]
    n = w.shape[1]
    assert m % tm == 0 and n % tn == 0 and w.shape[0] == k1 + k2
    return _dense_call(
        _mm2_res_kernel, (m // tm, n // tn),
        [pl.BlockSpec((tm, k1), lambda i, j: (i, 0)),
         pl.BlockSpec((tm, k2), lambda i, j: (i, 0)),
         pl.BlockSpec((k1 + k2, tn), lambda i, j: (0, j)),
         pl.BlockSpec((tm, tn), lambda i, j: (i, j))],
        [a1, a2, w, res],
        pl.BlockSpec((tm, tn), lambda i, j: (i, j)), jax.ShapeDtypeStruct((m, n), F32),
        [], cast, "matmul2_residual")


def _mmk_res_kernel(*refs, n_in, has_cast):
    (a_ref, w_ref, r_ref), o_ref, _ = _split_refs(refs, n_in, has_cast)

    @pl.when(pl.program_id(2) == 0)
    def _():
        o_ref[...] = r_ref[...]

    o_ref[...] += jnp.dot(a_ref[...], w_ref[...], preferred_element_type=F32)


def matmul_k_residual(a, w, res, *, tm, tn, tk, cast=None):
    m, k = a.shape
    n = w.shape[1]
    assert m % tm == 0 and n % tn == 0 and k % tk == 0
    return _dense_call(
        _mmk_res_kernel, (m // tm, n // tn, k // tk),
        [pl.BlockSpec((tm, tk), lambda i, j, l: (i, l)),
         pl.BlockSpec((tk, tn), lambda i, j, l: (l, j)),
         pl.BlockSpec((tm, tn), lambda i, j, l: (i, j))],
        [a, w, res],
        pl.BlockSpec((tm, tn), lambda i, j, l: (i, j)), jax.ShapeDtypeStruct((m, n), F32),
        [], cast, "matmul_k_residual")


def _ple_kernel(*refs, n_in, has_cast):
    ((x_hbm, g_ref, wg_ref, pe_ref, wp_ref, r_ref), o_ref,
     (xbuf_ref, sem_ref, xn_ref)) = _split_refs(refs, n_in, has_cast)
    _normalise_row_tile(x_hbm, g_ref, xbuf_ref, sem_ref, xn_ref)
    gate = jax.nn.sigmoid(jnp.dot(xn_ref[...], wg_ref[...], preferred_element_type=F32))
    proj = jnp.dot(pe_ref[...], wp_ref[...], preferred_element_type=F32)
    o_ref[...] = r_ref[...] + gate * proj


def ple_update(h, g, wg, pe, wp, layer, *, tm, tn, cast=None):
    m, k = h.shape
    n = wg.shape[1]
    kp = pe.shape[2]
    assert m % tm == 0 and n % tn == 0
    return _dense_call(
        _ple_kernel, (m // tm, n // tn),
        [pl.BlockSpec(memory_space=pl.ANY),
         pl.BlockSpec((1, k), lambda i, j: (0, 0)),
         pl.BlockSpec((k, tn), lambda i, j: (0, j)),
         pl.BlockSpec((None, tm, kp), lambda i, j: (layer, i, 0)),
         pl.BlockSpec((None, kp, tn), lambda i, j: (layer, 0, j)),
         pl.BlockSpec((tm, tn), lambda i, j: (i, j))],
        [h, g.reshape(1, k), wg, pe, wp, h],
        pl.BlockSpec((tm, tn), lambda i, j: (i, j)), jax.ShapeDtypeStruct((m, n), F32),
        _row_tile_scratch(tm, k), cast, "ple_update")


def _rmsnorm_kernel(x_ref, g_ref, o_ref):
    o_ref[...] = _rms_rows(x_ref[...], g_ref[...])


def rmsnorm(x, g, *, tm):
    m, k = x.shape
    assert m % tm == 0
    return pl.pallas_call(
        _rmsnorm_kernel,
        grid=(m // tm,),
        in_specs=[pl.BlockSpec((tm, k), lambda i: (i, 0)), pl.BlockSpec((1, k), lambda i: (0, 0))],
        out_specs=pl.BlockSpec((tm, k), lambda i: (i, 0)),
        out_shape=jax.ShapeDtypeStruct((m, k), F32),
        compiler_params=_cparams(1),
        name="final_rmsnorm",
    )(x, g.reshape(1, k))


def _pool_kernel(u_ref, buf_ref, w_ref, scale_ref, z_ref, full_ref, *, tt, pos0):
    t = pl.program_id(1)

    @pl.when(t == 0)
    def _():
        full_ref[0:POOL_CARRY, :] = buf_ref[0]

    @pl.when(t != 0)
    def _():
        full_ref[0:POOL_CARRY, :] = full_ref[tt:tt + POOL_CARRY, :]

    full_ref[POOL_CARRY:POOL_CARRY + tt, :] = u_ref[...]
    gc = w_ref.shape[1]
    pos = pos0 + t * tt + lax.broadcasted_iota(jnp.int32, (tt, 1), 0)
    for g, win in enumerate(POOL_WINDOWS):
        cols = slice(g * gc, (g + 1) * gc)
        cur = full_ref[POOL_CARRY:POOL_CARRY + tt, cols]
        tot = cur
        for i in range(1, win):
            tot = tot + full_ref[POOL_CARRY - i:POOL_CARRY - i + tt, cols]
        cnt = jnp.minimum(pos + 1, win).astype(F32)
        d = tot / cnt - cur
        z = jnp.dot(d.astype(BF16), w_ref[g], preferred_element_type=F32)
        z_ref[:, cols] = (z * scale_ref[:, cols]).astype(z_ref.dtype)


def pool_mixer(proj, buf16, w, layer, scale, *, bsz, t_len, tt, pos0):
    c = scale.shape[-1]
    nt = t_len // tt
    assert t_len % tt == 0 and (nt == 1 or tt >= POOL_CARRY)
    return pl.pallas_call(
        functools.partial(_pool_kernel, tt=tt, pos0=pos0),
        grid=(bsz, nt),
        in_specs=[
            pl.BlockSpec((tt, c), lambda b, t: (b * nt + t, 0)),
            pl.BlockSpec((1, POOL_CARRY, c), lambda b, t: (b, 0, 0)),
            pl.BlockSpec((None,) + w.shape[1:], lambda b, t: (layer, 0, 0, 0)),
            pl.BlockSpec((1, c), lambda b, t: (0, 0)),
        ],
        out_specs=pl.BlockSpec((tt, c), lambda b, t: (b * nt + t, 0)),
        out_shape=jax.ShapeDtypeStruct((bsz * t_len, c), BF16),
        scratch_shapes=[pltpu.VMEM((POOL_CARRY + tt, c), F32)],
        compiler_params=_cparams(2),
        name="pool_mixer",
    )(proj, buf16, w, scale.reshape(1, c))


def _lambda_value(lam_ref, lam_init):
    v = lam_ref[...]
    s1 = jnp.sum(v[0:1] * v[1:2], axis=-1, keepdims=True)
    s2 = jnp.sum(v[2:3] * v[3:4], axis=-1, keepdims=True)
    return jnp.exp(s1) - jnp.exp(s2) + lam_init


def _head_slopes(head_idx_f32):
    return jnp.exp2((-8.0 / DA_HEADS) * (head_idx_f32 + 1.0))


def _subln(o, g, lam_init):
    ms = jnp.mean(o * o, axis=-1, keepdims=True)
    return o * lax.rsqrt(ms + EPS) * g * (1.0 - lam_init)


LOG2E = math.log2(math.e)


def _softmax_tile(s, v_bf16, m_ref, l_ref, acc_ref, idx):
    cols = s.shape[1]
    m_prev = m_ref[idx]
    m_next = jnp.maximum(m_prev, jnp.max(s, axis=-1, keepdims=True))
    alpha = jnp.exp2(m_prev - m_next)
    p = jnp.exp2(s - jnp.tile(m_next, (1, cols // LANES)))
    part = p[:, 0:LANES]
    for i in range(1, cols // LANES):
        part = part + p[:, i * LANES:(i + 1) * LANES]
    l_ref[idx] = alpha * l_ref[idx] + part
    acc_ref[idx] = jnp.tile(alpha, (1, acc_ref.shape[-1] // LANES)) * acc_ref[idx] + jnp.dot(
        p.astype(BF16), v_bf16, preferred_element_type=F32)
    m_ref[idx] = m_next


def _attn_prompt_kernel(q_ref, k_ref, v_ref, lam_ref, g_ref, o_ref, kb_ref, vb_ref, m_ref, l_ref,
                        acc_ref, *, tq, lam_init):
    hd = DA_HEAD_DIM
    h = pl.program_id(1)
    qi = pl.program_id(2)

    @pl.when(qi == 0)
    def _():
        kb_ref[...] = k_ref[...].astype(BF16)
        vb_ref[...] = v_ref[...].astype(BF16)

    slope2 = _head_slopes(jnp.full((1, 1), h, jnp.int32).astype(F32)) * LOG2E
    q = (q_ref[...] * (hd ** -0.5 * LOG2E)).astype(BF16)
    m_ref[...] = jnp.full(m_ref.shape, NEG_INF, F32)
    l_ref[...] = jnp.zeros(l_ref.shape, F32)
    acc_ref[...] = jnp.zeros(acc_ref.shape, F32)
    k_local = lax.broadcasted_iota(jnp.int32, (1, tq), 1)

    def tile(ki, diagonal):
        start = pl.multiple_of(ki * tq, tq)
        k = kb_ref[pl.ds(start, tq), :]
        v = vb_ref[pl.ds(start, tq), :]
        col_bias = slope2 * (k_local + ki * tq).astype(F32)
        for c in range(2):
            s = lax.dot_general(q[:, c * hd:(c + 1) * hd], k[:, c * hd:(c + 1) * hd],
                                (((1,), (1,)), ((), ())), preferred_element_type=F32) + col_bias
            if diagonal:
                keep = (lax.broadcasted_iota(jnp.int32, (tq, tq), 0)
                        >= lax.broadcasted_iota(jnp.int32, (tq, tq), 1))
                s = jnp.where(keep, s, NEG_INF)
            _softmax_tile(s, v, m_ref, l_ref, acc_ref, c)

    def body(pair, carry):
        tile(2 * pair, False)
        tile(2 * pair + 1, False)
        return carry

    lax.fori_loop(0, qi // 2, body, 0)

    @pl.when(lax.rem(qi, 2) == 1)
    def _():
        tile(qi - 1, False)

    tile(qi, True)
    lam = _lambda_value(lam_ref, lam_init)
    o = (acc_ref[0] / jnp.sum(l_ref[0], axis=-1, keepdims=True)
         - lam * (acc_ref[1] / jnp.sum(l_ref[1], axis=-1, keepdims=True)))
    o_ref[...] = _subln(o, g_ref[...], lam_init).astype(o_ref.dtype)


def attention_prompt(proj, lam_vecs, subln_g, *, bsz, t_len, q_col, k_col, v_col, tq, lam_init):
    hw = 2 * DA_HEAD_DIM
    nq = t_len // tq
    assert t_len % tq == 0
    return pl.pallas_call(
        functools.partial(_attn_prompt_kernel, tq=tq, lam_init=lam_init),
        grid=(bsz, DA_HEADS, nq),
        in_specs=[
            pl.BlockSpec((tq, hw), lambda b, h, i: (b * nq + i, q_col + h)),
            pl.BlockSpec((t_len, hw), lambda b, h, i: (b, k_col + h)),
            pl.BlockSpec((t_len, hw), lambda b, h, i: (b, v_col + h)),
            pl.BlockSpec((4, DA_HEAD_DIM), lambda b, h, i: (0, 0)),
            pl.BlockSpec((1, hw), lambda b, h, i: (0, 0)),
        ],
        out_specs=pl.BlockSpec((tq, hw), lambda b, h, i: (b * nq + i, h)),
        out_shape=jax.ShapeDtypeStruct((bsz * t_len, DA_HEADS * hw), BF16),
        scratch_shapes=[pltpu.VMEM((t_len, hw), BF16), pltpu.VMEM((t_len, hw), BF16),
                        pltpu.VMEM((2, tq, LANES), F32), pltpu.VMEM((2, tq, LANES), F32),
                        pltpu.VMEM((2, tq, hw), F32)],
        compiler_params=_cparams(3),
        name="attention_prompt",
    )(proj, proj, proj, lam_vecs, subln_g.reshape(1, hw))


def _attn_sample_kernel(pt_ref, q_ref, *refs, n_pages, tpad, pos0, lam_init):
    del pt_ref
    k_refs = refs[:n_pages]
    v_refs = refs[n_pages:2 * n_pages]
    kn_ref, vn_ref, lam_ref, g_ref, o_ref, qall_ref, bias_ref, m_ref, l_ref, acc_ref = refs[2 * n_pages:]
    hd = DA_HEAD_DIM
    nh = DA_HEADS
    page = k_refs[0].shape[2]
    nrow = nh * 2 * tpad
    ncol = page * nh
    sidx = pl.program_id(1)
    sh_row = int(math.log2(2 * tpad))
    sh_col = int(math.log2(nh))
    assert 1 << sh_row == 2 * tpad and 1 << sh_col == nh

    def rel_and_match():
        row = lax.broadcasted_iota(jnp.int32, (nrow, ncol), 0)
        col = lax.broadcasted_iota(jnp.int32, (nrow, ncol), 1)
        rel = (jnp.right_shift(col, sh_col) - jnp.bitwise_and(row, tpad - 1)).astype(F32)
        match = jnp.right_shift(row, sh_row) == jnp.bitwise_and(col, nh - 1)
        return rel, match

    def row_slopes(shape):
        row = lax.broadcasted_iota(jnp.int32, shape, 0)
        return _head_slopes(jnp.right_shift(row, sh_row).astype(F32)) * LOG2E

    @pl.when(sidx == 0)
    def _():
        q = q_ref[...] * (hd ** -0.5 * LOG2E)
        lane = lax.broadcasted_iota(jnp.int32, (tpad, 2 * hd), 1)
        for h in range(nh):
            qh = q[:, h * 2 * hd:(h + 1) * 2 * hd]
            qall_ref[h * 2 * tpad:h * 2 * tpad + tpad, :] = jnp.where(lane < hd, qh, 0.0).astype(BF16)
            qall_ref[h * 2 * tpad + tpad:(h + 1) * 2 * tpad, :] = jnp.where(lane >= hd, qh, 0.0).astype(BF16)
        m_ref[...] = jnp.full(m_ref.shape, NEG_INF, F32)
        l_ref[...] = jnp.zeros(l_ref.shape, F32)
        acc_ref[...] = jnp.zeros(acc_ref.shape, F32)
        rel, match = rel_and_match()
        bias_ref[...] = jnp.where(match, row_slopes((nrow, ncol)) * rel, NEG_INF)

    def scores(k_page):
        kk = k_page.reshape(ncol, 2 * hd).astype(BF16)
        return lax.dot_general(qall_ref[...], kk, (((1,), (1,)), ((), ())), preferred_element_type=F32)

    slope_col = row_slopes((nrow, 1))
    for i in range(n_pages):
        base = ((sidx * n_pages + i) * page - pos0).astype(F32)
        s = scores(k_refs[i][0, 0]) + bias_ref[...] + slope_col * base
        vv = v_refs[i][0, 0].reshape(ncol, 2 * hd).astype(BF16)
        _softmax_tile(s, vv, m_ref, l_ref, acc_ref, 0)

    @pl.when(sidx == pl.num_programs(1) - 1)
    def _():
        rel, _ = rel_and_match()
        s = scores(kn_ref[0]) + jnp.where(rel <= 0.0, bias_ref[...], NEG_INF)
        vv = vn_ref[0].reshape(ncol, 2 * hd).astype(BF16)
        _softmax_tile(s, vv, m_ref, l_ref, acc_ref, 0)
        lam = _lambda_value(lam_ref, lam_init)
        a = acc_ref[0] / jnp.sum(l_ref[0], axis=-1, keepdims=True)
        for h in range(nh):
            r0 = h * 2 * tpad
            o = _subln(a[r0:r0 + tpad] - lam * a[r0 + tpad:r0 + 2 * tpad], g_ref[...], lam_init)
            o_ref[:, h * 2 * hd:(h + 1) * 2 * hd] = o.astype(o_ref.dtype)


def attention_sample(proj, cache_k, cache_v, layer, page_table, k_new, v_new, lam_vecs, subln_g,
                     *, bsz, tpad, q_col, n_pages, lam_init):
    hw = 2 * DA_HEAD_DIM
    qw = DA_HEADS * hw
    page = cache_k.shape[2]
    n_past_pages = page_table.shape[1]
    assert n_past_pages % n_pages == 0
    pos0 = n_past_pages * page
    nrow = DA_HEADS * 2 * tpad
    page_block = (1, 1, page, DA_HEADS, hw)

    def page_spec(i):
        return pl.BlockSpec(page_block, lambda b, s, pt: (layer, pt[b, s * n_pages + i], 0, 0, 0))

    new_spec = pl.BlockSpec((1, page, DA_HEADS, hw), lambda b, s, pt: (b, 0, 0, 0))
    grid_spec = pltpu.PrefetchScalarGridSpec(
        num_scalar_prefetch=1,
        grid=(bsz, n_past_pages // n_pages),
        in_specs=[pl.BlockSpec((tpad, qw), lambda b, s, pt: (b, q_col))]
        + [page_spec(i) for i in range(n_pages)] * 2
        + [new_spec, new_spec,
           pl.BlockSpec((4, DA_HEAD_DIM), lambda b, s, pt: (0, 0)),
           pl.BlockSpec((1, hw), lambda b, s, pt: (0, 0))],
        out_specs=pl.BlockSpec((tpad, qw), lambda b, s, pt: (b, 0)),
        scratch_shapes=[pltpu.VMEM((nrow, hw), BF16), pltpu.VMEM((nrow, page * DA_HEADS), F32),
                        pltpu.VMEM((1, nrow, LANES), F32), pltpu.VMEM((1, nrow, LANES), F32),
                        pltpu.VMEM((1, nrow, hw), F32)],
    )
    return pl.pallas_call(
        functools.partial(_attn_sample_kernel, n_pages=n_pages, tpad=tpad, pos0=pos0, lam_init=lam_init),
        grid_spec=grid_spec,
        out_shape=jax.ShapeDtypeStruct((bsz * tpad, qw), BF16),
        compiler_params=_cparams(2),
        name="attention_sample",
    )(page_table, proj, *([cache_k] * n_pages), *([cache_v] * n_pages), k_new, v_new,
      lam_vecs, subln_g.reshape(1, hw))


def _conformer_kernel(a_ref, gate_ref, buf_ref, dww_ref, dwb_ref, lng_ref, lnb_ref, pww_ref, pwb_ref,
                      z_ref, glu_ref, full_ref, cv_ref, *, tt):
    t = pl.program_id(1)

    @pl.when(t == 0)
    def _():
        full_ref[0:CONV_CARRY, :] = buf_ref[0]

    @pl.when(t != 0)
    def _():
        full_ref[0:CONV_CARRY, :] = full_ref[tt:tt + CONV_CARRY, :]

    glu = a_ref[...] * _sigmoid(gate_ref[...])
    full_ref[CONV_CARRY:CONV_CARRY + tt, :] = glu
    glu_ref[...] = glu
    first = CONV_CARRY - (CONV_K - 1)
    rb = min(tt, 128)
    phases = {}
    for k in range(CONV_K):
        phases.setdefault((first + k) % SUBLANES, []).append(k)

    def lane_chunk(c, carry):
        col = pl.ds(pl.multiple_of(c * LANES, LANES), LANES)
        for r0 in range(0, tt, rb):
            acc = jnp.zeros((rb, LANES), F32) + dwb_ref[:, col]
            for phase, taps in phases.items():
                rows = rb if phase == 0 else rb + SUBLANES
                part = None
                for k in taps:
                    base = r0 + first + k - phase
                    term = dww_ref[k:k + 1, col] * full_ref[base:base + rows, col]
                    part = term if part is None else part + term
                acc = acc + part[phase:phase + rb]
            cv_ref[r0:r0 + rb, col] = acc
        return carry

    lax.fori_loop(0, cv_ref.shape[1] // LANES, lane_chunk, 0)
    y = cv_ref[...]
    mu = jnp.mean(y, axis=-1, keepdims=True)
    yc = y - mu
    var = jnp.mean(yc * yc, axis=-1, keepdims=True)
    y = yc * lax.rsqrt(var + EPS) * lng_ref[...] + lnb_ref[...]
    y = y * _sigmoid(y)
    z = jnp.dot(y.astype(BF16), pww_ref[...], preferred_element_type=F32) + pwb_ref[...]
    z_ref[...] = z.astype(z_ref.dtype)


def conformer_mixer(proj, buf32, dw_w, dw_b, ln_g, ln_b, pw_w, layer, pw_b, *, bsz, t_len, tt):
    c = dw_b.shape[-1]
    nt = t_len // tt
    assert t_len % tt == 0 and (nt == 1 or tt >= CONV_CARRY)
    vec = lambda: pl.BlockSpec((1, c), lambda b, t: (0, 0))
    rows = lambda j: pl.BlockSpec((tt, c), lambda b, t: (b * nt + t, j))
    return pl.pallas_call(
        functools.partial(_conformer_kernel, tt=tt),
        grid=(bsz, nt),
        in_specs=[
            rows(0), rows(1),
            pl.BlockSpec((1, CONV_CARRY, c), lambda b, t: (b, 0, 0)),
            pl.BlockSpec((CONV_K, c), lambda b, t: (0, 0)),
            vec(), vec(), vec(),
            pl.BlockSpec((None, c, c), lambda b, t: (layer, 0, 0)),
            vec(),
        ],
        out_specs=[rows(0), rows(0)],
        out_shape=[jax.ShapeDtypeStruct((bsz * t_len, c), BF16),
                   jax.ShapeDtypeStruct((bsz * t_len, c), F32)],
        scratch_shapes=[pltpu.VMEM((CONV_CARRY + tt, c), F32), pltpu.VMEM((tt, c), F32)],
        compiler_params=_cparams(2),
        name="conformer_mixer",
    )(proj, proj, buf32, dw_w, dw_b.reshape(1, c), ln_g.reshape(1, c), ln_b.reshape(1, c),
      pw_w, pw_b.reshape(1, c))


def _split3(x, axis):
    hi = x.astype(BF16)
    rest = x - hi.astype(F32)
    mid = rest.astype(BF16)
    lo = (rest - mid.astype(F32)).astype(BF16)
    return jnp.concatenate([hi, mid, lo], axis=axis)


def _select_dot_left(sel_bf16, x):
    n = x.shape[1]
    r = jnp.dot(sel_bf16, _split3(x, 1), preferred_element_type=F32)
    return r[:, 0:n] + r[:, n:2 * n] + r[:, 2 * n:3 * n]


def _ssd_kernel(z_ref, x_ref, b_ref, c_ref, dt_ref, buf_ref, cw_ref, cb_ref, dtb_ref, alog_ref,
                dexp_ref, ng_ref, e_ref, et_ref, h0_ref, y_ref, hn_ref, full_ref, st_ref,
                *, t_valid):
    t = pl.program_id(1)
    ln = x_ref.shape[0]
    inner = x_ref.shape[1]
    gn = b_ref.shape[1]
    ns = SSM_STATE
    gw = inner // SSM_GROUPS
    hpg = gw // SSM_HEAD_DIM

    @pl.when(t == 0)
    def _():
        full_ref[0:SSM_CARRY, :] = buf_ref[0]
        st_ref[...] = h0_ref[0]

    @pl.when(t != 0)
    def _():
        full_ref[0:SSM_CARRY, :] = full_ref[ln:ln + SSM_CARRY, :]

    full_ref[SSM_CARRY:SSM_CARRY + ln, 0:inner] = x_ref[...]
    full_ref[SSM_CARRY:SSM_CARRY + ln, inner:inner + gn] = b_ref[...]
    full_ref[SSM_CARRY:SSM_CARRY + ln, inner + gn:inner + 2 * gn] = c_ref[...]
    first = SSM_CARRY - (SSM_CONV_K - 1)
    xc = cb_ref[...] + cw_ref[0:1, :] * full_ref[first:first + ln, :]
    for k in range(1, SSM_CONV_K):
        xc = xc + cw_ref[k:k + 1, :] * full_ref[first + k:first + k + ln, :]
    xc = xc * _sigmoid(xc)
    xs = xc[:, 0:inner]
    bm = xc[:, inner:inner + gn].astype(BF16)
    cm = xc[:, inner + gn:inner + 2 * gn].astype(BF16)

    dt_in = dt_ref[...] + dtb_ref[...]
    dt = jnp.maximum(dt_in, 0.0) + jnp.log1p(jnp.exp(-jnp.abs(dt_in)))
    row = lax.broadcasted_iota(jnp.int32, (ln, ln), 0)
    col = lax.broadcasted_iota(jnp.int32, (ln, ln), 1)
    if t_valid < ln:
        dt = jnp.where(lax.broadcasted_iota(jnp.int32, dt.shape, 0) < t_valid, dt, 0.0)
    ad = dt * (-jnp.exp(alog_ref[...]))
    causal = row >= col
    a_cs = _select_dot_left(jnp.where(causal, 1.0, 0.0).astype(BF16), ad)
    a_cs_t = a_cs.T
    a_last = a_cs[ln - 1:ln, :]
    per_head = jnp.concatenate([dt, jnp.exp(a_last - a_cs), jnp.exp(a_cs)], axis=0)
    per_chan = jnp.dot(_split3(per_head, 1), e_ref[...], preferred_element_type=F32)
    xd = xs * per_chan[0:ln]
    xdw = (xd * per_chan[ln:2 * ln]).astype(BF16)
    xd = xd.astype(BF16)
    exp_acs = per_chan[2 * ln:3 * ln]
    last_t = jnp.broadcast_to(jnp.exp(a_cs_t[:, ln - 1:ln]), a_cs_t.shape)
    st_decay = _select_dot_left(et_ref[...], last_t)

    lane_head = lax.broadcasted_iota(jnp.int32, (ln, gw), 1) // SSM_HEAD_DIM
    for g in range(SSM_GROUPS):
        cols = slice(g * gw, (g + 1) * gw)
        scols = slice(g * ns, (g + 1) * ns)
        scores = lax.dot_general(cm[:, scols], bm[:, scols], (((1,), (1,)), ((), ())),
                                 preferred_element_type=F32)
        lhs, rhs = [], []
        for r in range(hpg):
            h = g * hpg + r
            seg = a_cs[:, h:h + 1] - a_cs_t[h:h + 1, :]
            lhs.append((scores * jnp.where(causal, jnp.exp(seg), 0.0)).astype(BF16))
            rhs.append(jnp.where(lane_head == r, xd[:, cols], jnp.zeros_like(xd[:, cols])))
        y = jnp.dot(jnp.concatenate(lhs, axis=1), jnp.concatenate(rhs, axis=0),
                    preferred_element_type=F32)
        h_prev = st_ref[cols, :]
        y = y + exp_acs[:, cols] * lax.dot_general(
            cm[:, scols], h_prev.astype(BF16), (((1,), (1,)), ((), ())), preferred_element_type=F32)
        new_states = lax.dot_general(xdw[:, cols], bm[:, scols], (((0,), (0,)), ((), ())),
                                     preferred_element_type=F32)
        st_ref[cols, :] = h_prev * st_decay[cols, :] + new_states
        y = y + dexp_ref[:, cols] * xs[:, cols]
        zg = z_ref[:, cols]
        y = y * (zg * _sigmoid(zg))
        y_ref[:, cols] = (_rms_rows(y, ng_ref[:, cols])).astype(y_ref.dtype)

    @pl.when(t == pl.num_programs(1) - 1)
    def _():
        hn_ref[0] = st_ref[...]


def ssd_mixer(proj, dt_raw, buf8, h0, conv_w, conv_b, dt_bias, a_log, d_skip, norm_g,
              *, bsz, t_len, t_valid, z_col, x_col, b_col, c_col):
    inner = norm_g.shape[-1]
    heads = inner // SSM_HEAD_DIM
    gn = SSM_GROUPS * SSM_STATE
    xbc = inner + 2 * gn
    ln = SSM_CHUNK
    nc = t_len // ln
    assert t_len % ln == 0 and heads <= LANES

    def pad_heads(v):
        return jnp.pad(v.astype(F32), (0, LANES - heads)).reshape(1, LANES)

    expand = (jnp.arange(LANES)[:, None] == (jnp.arange(inner) // SSM_HEAD_DIM)[None, :]).astype(BF16)
    d_exp = jnp.repeat(d_skip.astype(F32), SSM_HEAD_DIM).reshape(1, inner)
    const = lambda shape: pl.BlockSpec(shape, lambda b, t: (0,) * len(shape))
    rows = lambda w, j: pl.BlockSpec((ln, w), lambda b, t: (b * nc + t, j))
    return pl.pallas_call(
        functools.partial(_ssd_kernel, t_valid=t_valid),
        grid=(bsz, nc),
        in_specs=[
            rows(inner, z_col), rows(inner, x_col), rows(gn, b_col), rows(gn, c_col),
            rows(LANES, 0),
            pl.BlockSpec((1, SSM_CARRY, xbc), lambda b, t: (b, 0, 0)),
            const((SSM_CONV_K, xbc)), const((1, xbc)),
            const((1, LANES)), const((1, LANES)),
            const((1, inner)), const((1, inner)),
            const((3 * LANES, inner)), const((inner, LANES)),
            pl.BlockSpec((1, inner, SSM_STATE), lambda b, t: (b, 0, 0)),
        ],
        out_specs=[rows(inner, 0), pl.BlockSpec((1, inner, SSM_STATE), lambda b, t: (b, 0, 0))],
        out_shape=[jax.ShapeDtypeStruct((bsz * t_len, inner), BF16),
                   jax.ShapeDtypeStruct((bsz, inner, SSM_STATE), F32)],
        scratch_shapes=[pltpu.VMEM((SSM_CARRY + ln, xbc), F32), pltpu.VMEM((inner, SSM_STATE), F32)],
        compiler_params=_cparams(2),
        name="ssd_mixer",
    )(proj, proj, proj, proj, dt_raw, buf8, conv_w, conv_b.reshape(1, xbc), pad_heads(dt_bias),
      pad_heads(a_log), d_exp, norm_g.reshape(1, inner), jnp.tile(expand, (3, 1)), expand.T, h0)


def _kv_layout_kernel(*refs, n_layers):
    ins = refs[:2 * n_layers]
    ko_ref, vo_ref = refs[2 * n_layers:]
    hw = ko_ref.shape[-1]
    for j in range(n_layers):
        @pl.when(pl.program_id(0) == j)
        def _(j=j):
            for h in range(DA_HEADS):
                ko_ref[0, 0, :, h, :] = ins[2 * j][:, h * hw:(h + 1) * hw]
                vo_ref[0, 0, :, h, :] = ins[2 * j + 1][:, h * hw:(h + 1) * hw]


def kv_layout(projs, *, bsz, t_len, tt, k_col, v_col):
    n_layers = len(projs)
    hw = 2 * DA_HEAD_DIM
    width = DA_HEADS * hw
    nt = t_len // tt
    assert t_len % tt == 0

    def src(j, col):
        return pl.BlockSpec((tt, width), lambda l, b, t: (jnp.where(l == j, b * nt + t, 0), col))

    out_spec = pl.BlockSpec((1, 1, tt, DA_HEADS, hw), lambda l, b, t: (l, b, t, 0, 0))
    out_shape = jax.ShapeDtypeStruct((n_layers, bsz, t_len, DA_HEADS, hw), F32)
    in_specs, args = [], []
    for j, p in enumerate(projs):
        in_specs += [src(j, k_col), src(j, v_col)]
        args += [p, p]
    return pl.pallas_call(
        functools.partial(_kv_layout_kernel, n_layers=n_layers),
        grid=(n_layers, bsz, nt),
        in_specs=in_specs,
        out_specs=[out_spec, out_spec],
        out_shape=[out_shape, out_shape],
        compiler_params=_cparams(3),
        name="kv_layout",
    )(*args)


def _tail_rows(buf, rows, cols, t_valid, n):
    take = min(n, t_valid)
    new = rows[:, t_valid - take:t_valid, cols]
    if take == n:
        return new
    return jnp.concatenate([buf[:, buf.shape[1] - (n - take):], new], axis=1)


def _pad_front(buf, rows):
    return jnp.pad(buf, ((0, 0), (rows - buf.shape[1], 0), (0, 0)))


def _dt_weight_kernel(src_ref, o_ref):
    rows = src_ref[...]
    pad = jnp.zeros((LANES - rows.shape[0], rows.shape[1]), F32)
    o_ref[...] = jnp.concatenate([rows, pad], axis=0).T.astype(BF16)


def dt_weight(w_t, row0, n_rows):
    n_layers, _, k = w_t.shape
    assert row0 % n_rows == 0 and n_rows % SUBLANES == 0 and n_rows <= LANES
    return pl.pallas_call(
        _dt_weight_kernel,
        grid=(n_layers,),
        in_specs=[pl.BlockSpec((None, n_rows, k), lambda l: (l, row0 // n_rows, 0))],
        out_specs=pl.BlockSpec((None, k, LANES), lambda l: (l, 0, 0)),
        out_shape=jax.ShapeDtypeStruct((n_layers, k, LANES), BF16),
        compiler_params=_cparams(1),
        name="dt_weight",
    )(w_t)


class MatmulWeights:
    def __init__(self, depth, sources, ready):
        self.depth = depth
        self._sources = sources
        self.ready = ready

    def get(self, name, layer):
        return self.ready[name][layer]

    def pending(self, name, layer):
        if layer >= self.depth or self.ready[name][layer] is not None:
            return None
        return self._sources[name](layer)

    def run(self, fn, target, *args, **kwargs):
        cast = self.pending(*target) if target is not None else None
        res = fn(*args, cast=cast, **kwargs)
        if cast is None:
            return res
        out, self.ready[target[0]][target[1]] = res
        return out


def _trunk(x, pemb, wts, mmw, states, *, bsz, t_len, t_valid, tm, tn, tt, paged):
    depth = pemb.shape[0]
    d_model = x.shape[1]
    pool_w = d_model // 2
    da_w = DA_HEADS * 2 * DA_HEAD_DIM
    hw = 2 * DA_HEAD_DIM
    conv_w = d_model // 2
    inner = d_model // 2
    gn = SSM_GROUPS * SSM_STATE
    xbc_w = inner + 2 * gn
    tm2 = min(2 * tm, bsz * t_len)
    pos0 = 0 if paged is None else paged[2].shape[1] * paged[0].shape[2]
    h = x
    outs = dict(k=[], v=[], pool=[], conf=[], mconv=[], ssm=[])
    even_projs = []
    q_col = pool_w // hw

    def pad_seq_rows(a, t_new):
        return jnp.pad(a.reshape(bsz, t_len, -1),
                       ((0, 0), (0, t_new - t_len), (0, 0))).reshape(bsz * t_new, -1)

    for i in range(depth):
        j = i // 2
        if i % 2 == 0:
            proj = mmw.run(norm_matmul, ("w_out", i), h, wts["g_mix"][i], mmw.get("w_in", i),
                           tm=tm, tn=tn)
            rows = proj.reshape(bsz, t_len, -1)
            pool_buf = states["pool"][j]
            outs["pool"].append(_tail_rows(pool_buf, rows, slice(0, pool_w), t_valid, POOL_BUF))
            z_pool = pool_mixer(proj, _pad_front(pool_buf, POOL_CARRY), wts["pool_w"], j,
                                wts["pool_scale"][j], bsz=bsz, t_len=t_len, tt=tt, pos0=pos0)
            lam_init = 0.8 - 0.6 * math.exp(-0.3 * i)
            lam_vecs = jnp.stack([wts["lambda_q1"][j], wts["lambda_k1"][j],
                                  wts["lambda_q2"][j], wts["lambda_k2"][j]]).astype(F32)
            if paged is None:
                even_projs.append(proj)
                attn = attention_prompt(
                    proj, lam_vecs, wts["subln_g"][j], bsz=bsz, t_len=t_len, q_col=q_col,
                    k_col=q_col + DA_HEADS, v_col=q_col + 2 * DA_HEADS, tq=512, lam_init=lam_init)
            else:
                cache_k, cache_v, page_table = paged
                page = cache_k.shape[2]
                k_seq = rows[:, :t_valid, pool_w + da_w:pool_w + 2 * da_w].reshape(
                    bsz, t_valid, DA_HEADS, hw)
                v_seq = rows[:, :t_valid, pool_w + 2 * da_w:pool_w + 3 * da_w].reshape(
                    bsz, t_valid, DA_HEADS, hw)
                outs["k"].append(k_seq)
                outs["v"].append(v_seq)
                new_page = lambda r: jnp.pad(r, ((0, 0), (0, page - t_valid), (0, 0), (0, 0)))
                attn = attention_sample(
                    proj, cache_k, cache_v, j, page_table, new_page(k_seq), new_page(v_seq),
                    lam_vecs, wts["subln_g"][j], bsz=bsz, tpad=t_len, q_col=pool_w // da_w,
                    n_pages=8, lam_init=lam_init)
            h = mmw.run(matmul2_residual, ("w_gate", i), z_pool, attn, mmw.get("w_out", i), h,
                        tm=tm2, tn=tn // 2)
        else:
            x0 = 2 * conv_w + inner
            proj = mmw.run(norm_matmul, ("w_out", i), h, wts["g_mix"][i], mmw.get("w_in", i),
                           tm=tm, tn=tn)
            dt_raw = norm_matmul(h, wts["g_mix"][i], wts["w_dt"][j], tm=tm, tn=LANES)
            rows = proj.reshape(bsz, t_len, -1)
            conf_buf = states["conf"][j]
            zc, glu = conformer_mixer(
                proj, _pad_front(conf_buf, CONV_CARRY), wts["conf_dw_w"][j], wts["conf_dw_b"][j],
                wts["conf_ln_g"][j], wts["conf_ln_b"][j], wts["conf_pw_w"], j, wts["conf_pw_b"][j],
                bsz=bsz, t_len=t_len, tt=tt)
            outs["conf"].append(_tail_rows(conf_buf, glu.reshape(bsz, t_len, conv_w),
                                           slice(None), t_valid, CONV_K - 1))
            mconv_buf = states["mconv"][j]
            outs["mconv"].append(_tail_rows(mconv_buf, rows, slice(x0, x0 + xbc_w), t_valid,
                                            SSM_CONV_K - 1))
            t_ssd = -(-t_len // SSM_CHUNK) * SSM_CHUNK
            proj_ssd = proj
            if t_ssd != t_len:
                proj_ssd = pad_seq_rows(proj, t_ssd)
                dt_raw = pad_seq_rows(dt_raw, t_ssd)
            y, h_new = ssd_mixer(
                proj_ssd, dt_raw, _pad_front(mconv_buf, SSM_CARRY), states["ssm"][j],
                wts["ssm_conv_w"][j], wts["ssm_conv_b"][j], wts["ssm_dt_bias"][j],
                wts["ssm_A_log"][j], wts["ssm_D"][j], wts["ssm_norm_g"][j],
                bsz=bsz, t_len=t_ssd, t_valid=min(t_valid, SSM_CHUNK),
                z_col=2 * conv_w // inner, x_col=x0 // inner, b_col=(x0 + inner) // gn,
                c_col=(x0 + inner + gn) // gn)
            if t_ssd != t_len:
                y = y.reshape(bsz, t_ssd, inner)[:, :t_len].reshape(bsz * t_len, inner)
            outs["ssm"].append(h_new)
            h = mmw.run(matmul2_residual, ("w_gate", i), zc, y, mmw.get("w_out", i), h,
                        tm=tm2, tn=tn // 2)
        hidden = mmw.run(norm_matmul, ("w_down", i), h, wts["g_mlp"][i], mmw.get("w_up", i),
                         tm=tm, tn=tn, act="relu2", out_dtype=BF16)
        h = mmw.run(matmul_k_residual, ("w_up", i + 1), hidden, mmw.get("w_down", i), h,
                    tm=tm2, tn=tn, tk=2 * tn)
        h = mmw.run(ple_update, ("w_in", i + 1), h, wts["g_ple"][i], mmw.get("w_gate", i), pemb,
                    wts["w_ple_proj"], i, tm=tm, tn=tn // 2)
    if paged is None:
        outs["k"], outs["v"] = kv_layout(even_projs, bsz=bsz, t_len=t_len, tt=tt,
                                         k_col=pool_w // da_w + 1, v_col=pool_w // da_w + 2)
    else:
        outs["k"], outs["v"] = jnp.stack(outs["k"]), jnp.stack(outs["v"])
    y = rmsnorm(h, wts["g_final"], tm=min(256, bsz * t_len))
    return y, outs


def kernel(x_prompt, x_sample, p_prompt, p_sample, cache_k, cache_v, page_table, state_pool, state_conf_conv, state_ssm_conv, state_ssm, g_mix, g_mlp, g_ple, g_final, w_in_even, pool_w, pool_scale, lambda_q1, lambda_k1, lambda_q2, lambda_k2, subln_g, w_out_even, w_in_odd, conf_dw_w, conf_dw_b, conf_ln_g, conf_ln_b, conf_pw_w, conf_pw_b, ssm_conv_w, ssm_conv_b, ssm_dt_bias, ssm_A_log, ssm_D, ssm_norm_g, w_out_odd, w_up, w_down, w_ple_proj, w_ple_gate):
    bp, tp, d_model = x_prompt.shape
    bs, ts, _ = x_sample.shape
    depth = p_prompt.shape[0]
    n_even, n_odd = (depth + 1) // 2, depth // 2
    inner = ssm_norm_g.shape[-1]
    n_main = w_in_odd.shape[-1] - inner // SSM_HEAD_DIM
    w_in_odd_t = jnp.swapaxes(w_in_odd, 1, 2)
    w_dt = dt_weight(w_in_odd_t, n_main, w_in_odd.shape[-1] - n_main)
    wts = dict(
        w_dt=w_dt,
        g_mix=g_mix, g_mlp=g_mlp, g_ple=g_ple, g_final=g_final,
        pool_w=pool_w.astype(BF16), pool_scale=pool_scale,
        lambda_q1=lambda_q1, lambda_k1=lambda_k1, lambda_q2=lambda_q2, lambda_k2=lambda_k2,
        subln_g=subln_g,
        conf_dw_w=conf_dw_w, conf_dw_b=conf_dw_b, conf_ln_g=conf_ln_g, conf_ln_b=conf_ln_b,
        conf_pw_w=conf_pw_w.astype(BF16), conf_pw_b=conf_pw_b, ssm_conv_w=ssm_conv_w,
        ssm_conv_b=ssm_conv_b, ssm_dt_bias=ssm_dt_bias, ssm_A_log=ssm_A_log, ssm_D=ssm_D,
        ssm_norm_g=ssm_norm_g, w_ple_proj=w_ple_proj.astype(BF16))
    sources = dict(
        w_in=lambda i: (WeightCast(w_in_even, i // 2, w_in_even.shape[-1]) if i % 2 == 0
                        else WeightCast(w_in_odd_t, i // 2, n_main, transposed=True)),
        w_out=lambda i: WeightCast(w_out_even if i % 2 == 0 else w_out_odd, i // 2, d_model),
        w_up=lambda i: WeightCast(w_up, i, w_up.shape[-1]),
        w_down=lambda i: WeightCast(w_down, i, d_model),
        w_gate=lambda i: WeightCast(w_ple_gate, i, d_model))
    ready = {name: [None] * depth for name in sources}
    ready["w_in"][0] = w_in_even[0].astype(BF16)
    ready["w_up"][0] = w_up[0].astype(BF16)
    mmw = MatmulWeights(depth, sources, ready)
    dtype = x_prompt.dtype

    zero_states = dict(
        pool=jnp.zeros((n_even, bp, POOL_BUF, pool_scale.shape[-1]), dtype),
        conf=jnp.zeros((n_odd, bp, CONV_K - 1, conf_dw_b.shape[-1]), dtype),
        mconv=jnp.zeros((n_odd, bp, SSM_CONV_K - 1, ssm_conv_b.shape[-1]), dtype),
        ssm=jnp.zeros((n_odd, bp, inner, SSM_STATE), F32))
    y_p, o_p = _trunk(
        x_prompt.reshape(bp * tp, d_model), p_prompt.reshape(depth, bp * tp, -1).astype(BF16), wts,
        mmw, zero_states, bsz=bp, t_len=tp, t_valid=tp, tm=512, tn=1024, tt=256, paged=None)

    t_pad = SAMPLE_T_PAD
    pad_t = lambda a, axis: jnp.pad(a, [(0, t_pad - ts) if d == axis else (0, 0) for d in range(a.ndim)])
    sample_states = dict(pool=state_pool, conf=state_conf_conv, mconv=state_ssm_conv,
                         ssm=state_ssm.astype(F32).reshape(n_odd, bs, inner, SSM_STATE))
    y_s, o_s = _trunk(
        pad_t(x_sample, 1).reshape(bs * t_pad, d_model),
        pad_t(p_sample, 2).reshape(depth, bs * t_pad, -1).astype(BF16), wts, mmw, sample_states,
        bsz=bs, t_len=t_pad, t_valid=ts, tm=bs * t_pad, tn=2048, tt=t_pad,
        paged=(cache_k, cache_v, page_table))

    def ssm_out(states, bsz):
        return jnp.stack(states).reshape(n_odd, bsz, inner // SSM_HEAD_DIM, SSM_HEAD_DIM,
                                         SSM_STATE).astype(state_ssm.dtype)

    return (y_p.reshape(bp, tp, d_model), y_s.reshape(bs, t_pad, d_model)[:, :ts],
            o_p["k"], o_p["v"], o_s["k"], o_s["v"],
            jnp.stack(o_p["pool"]), jnp.stack(o_s["pool"]),
            jnp.stack(o_p["conf"]), jnp.stack(o_s["conf"]),
            jnp.stack(o_p["mconv"]), jnp.stack(o_s["mconv"]),
            ssm_out(o_p["ssm"], bp), ssm_out(o_s["ssm"], bs))
```
